```python
import jax
import jax.numpy as jnp
from jax import lax
import numpy as np

D_MODEL = 1024
BATCH = 32
SEQ = 2048
DEPTH = 1
DEC_BATCH = 16
DEC_SEQ = 2048
PAST_LEN = 128

N_HEADS_GLA = 4
DV_GLA = D_MODEL // N_HEADS_GLA
DK_GLA = DV_GLA // 2
GLA_LOWRANK = 16
GLA_TAU = 16.0
N_HEADS_ML = 4
DV_ML = D_MODEL // N_HEADS_ML
DK_ML = DV_ML // 2
CONV_W = 3
D_FF = 2816
CHUNK = 64
EPS = 1e-6

GLA_QK = N_HEADS_GLA * DK_GLA
GLA_V = N_HEADS_GLA * DV_GLA
ML_QK = N_HEADS_ML * DK_ML
ML_V = N_HEADS_ML * DV_ML
IN_SIZES = (GLA_QK, GLA_QK, GLA_V, GLA_V, 2 * GLA_LOWRANK, 2 * ML_QK, ML_V, ML_V, 4 * N_HEADS_ML, D_MODEL, D_MODEL)
IN_OFFSETS = tuple(int(o) for o in np.cumsum(IN_SIZES)[:-1])
D_IN = int(sum(IN_SIZES))

kernel_name = "hybrid_gla_mlstm_macaron_encoder"


def rmsnorm(x, g):
    xf = x.astype(jnp.float32)
    y = xf * lax.rsqrt(jnp.mean(xf * xf, axis=-1, keepdims=True) + EPS)
    return (y * g.astype(jnp.float32)).astype(x.dtype)


def head_rmsnorm(o, g):
    B, L, H, Dv = o.shape
    y = o * lax.rsqrt(jnp.mean(o * o, axis=-1, keepdims=True) + EPS)
    return y.reshape(B, L, H * Dv) * g.astype(jnp.float32)


def swiglu(h, w_in, w_out):
    a, g = jnp.split(h @ w_in, [D_FF], axis=-1)
    return (jax.nn.silu(a) * g) @ w_out


def to_chunks(t):
    B, L, H = t.shape[:3]
    t = t.reshape((B, L // CHUNK, CHUNK, H) + t.shape[3:])
    return jnp.moveaxis(t, (1, 3), (0, 2))


def from_chunks(t):
    t = jnp.moveaxis(t, (0, 2), (1, 3))
    B, nc, C, H = t.shape[:4]
    return t.reshape((B, nc * C, H) + t.shape[4:])


def flip(t):
    return jnp.flip(t, axis=1)


def gla_scan(q, k, v, log_a):
    qc, kc, vc, gc = (to_chunks(t) for t in (q, k, v, log_a))
    b = jnp.cumsum(gc, axis=-2)
    b_last = b[..., -1, :]
    q_in = qc * jnp.exp(b)
    k_in = kc * jnp.exp(-b)
    k_st = kc * jnp.exp(b_last[..., None, :] - b)
    mask = jnp.tril(jnp.ones((CHUNK, CHUNK), dtype=bool))

    def step(S, xs):
        qi, ki, ks, vi, bl = xs
        att = jnp.where(mask, jnp.einsum('bhtd,bhsd->bhts', qi, ki), 0.0)
        o = jnp.einsum('bhts,bhsv->bhtv', att, vi) + jnp.einsum('bhtd,bhdv->bhtv', qi, S)
        S = S * jnp.exp(bl)[..., None] + jnp.einsum('bhsd,bhsv->bhdv', ks, vi)
        return S, o

    B, _, H, DK = q.shape
    S0 = jnp.zeros((B, H, DK, v.shape[-1]), jnp.float32)
    _, o = lax.scan(step, S0, (q_in, k_in, k_st, vc, b_last))
    return from_chunks(o)


def gla_mixer(q, k, v, r, lr, w2_f, b2_f, w2_b, b2_b, g_norm):
    B, L, _ = q.shape
    f32 = jnp.float32
    qh = q.astype(f32).reshape(B, L, N_HEADS_GLA, DK_GLA) * (DK_GLA ** -0.5)
    kh = k.astype(f32).reshape(B, L, N_HEADS_GLA, DK_GLA)
    vh = v.astype(f32).reshape(B, L, N_HEADS_GLA, DV_GLA)
    lr_f, lr_b = jnp.split(lr, [GLA_LOWRANK], axis=-1)
    la_f = jax.nn.log_sigmoid((lr_f @ w2_f + b2_f).astype(f32)).reshape(B, L, N_HEADS_GLA, DK_GLA) / GLA_TAU
    la_b = jax.nn.log_sigmoid((lr_b @ w2_b + b2_b).astype(f32)).reshape(B, L, N_HEADS_GLA, DK_GLA) / GLA_TAU
    o = gla_scan(qh, kh, vh, la_f) + flip(gla_scan(flip(qh), flip(kh), flip(vh), flip(la_b)))
    return head_rmsnorm(o, g_norm) * jax.nn.silu(r.astype(f32))


def mlstm_scan(q, k, v, ig, lf):
    qc, kc, vc, ic, fc = (to_chunks(t) for t in (q, k, v, ig, lf))
    b = jnp.cumsum(fc, axis=-1)
    mask = jnp.tril(jnp.ones((CHUNK, CHUNK), dtype=bool))

    def step(carry, xs):
        Cm, n, m = carry
        qi, ki, vi, ii, bi = xs
        logD = jnp.where(mask, bi[..., :, None] - bi[..., None, :] + ii[..., None, :], -jnp.inf)
        m_inter = bi + m[..., None]
        m_t = jnp.maximum(m_inter, jnp.max(logD, axis=-1))
        Dm = jnp.exp(logD - m_t[..., None])
        w_inter = jnp.exp(m_inter - m_t)
        s = jnp.einsum('bhtd,bhsd->bhts', qi, ki) * Dm
        num = jnp.einsum('bhts,bhsv->bhtv', s, vi) + w_inter[..., None] * jnp.einsum('bhtd,bhdv->bhtv', qi, Cm)
        den = jnp.sum(s, axis=-1) + w_inter * jnp.einsum('bhtd,bhd->bht', qi, n)
        h = num / jnp.maximum(jnp.abs(den), jnp.exp(-m_t))[..., None]
        b_L = bi[..., -1]
        log_w_st = b_L[..., None] - bi + ii
        m_new = jnp.maximum(b_L + m, jnp.max(log_w_st, axis=-1))
        w_st = jnp.exp(log_w_st - m_new[..., None])
        decay = jnp.exp(b_L + m - m_new)
        Cm = decay[..., None, None] * Cm + jnp.einsum('bhs,bhsd,bhsv->bhdv', w_st, ki, vi)
        n = decay[..., None] * n + jnp.einsum('bhs,bhsd->bhd', w_st, ki)
        return (Cm, n, m_new), h

    B, _, H, DK = q.shape
    init = (jnp.zeros((B, H, DK, v.shape[-1]), jnp.float32),
            jnp.zeros((B, H, DK), jnp.float32),
            jnp.full((B, H), -jnp.inf, jnp.float32))
    _, h = lax.scan(step, init, (qc, kc, vc, ic, b))
    return from_chunks(h)


def mlstm_mixer(qk, v, o, if_pre, conv_w, conv_b, b_if, g_norm):
    B, L, _ = qk.shape
    f32 = jnp.float32
    pad = CONV_W // 2
    qkp = jnp.pad(qk, ((0, 0), (pad, pad), (0, 0)))
    conv = conv_b
    for j in range(CONV_W):
        conv = conv + qkp[:, j:j + L] * conv_w[j]
    qk_c = jax.nn.silu(conv.astype(f32))
    q, k = jnp.split(qk_c, [ML_QK], axis=-1)
    qh = q.reshape(B, L, N_HEADS_ML, DK_ML) * (DK_ML ** -0.5)
    kh = k.reshape(B, L, N_HEADS_ML, DK_ML)
    vh = v.astype(f32).reshape(B, L, N_HEADS_ML, DV_ML)
    gates = (if_pre + b_if).astype(f32).reshape(B, L, 4, N_HEADS_ML)
    ig_f, lf_f = gates[:, :, 0], jax.nn.log_sigmoid(gates[:, :, 1])
    ig_b, lf_b = gates[:, :, 2], jax.nn.log_sigmoid(gates[:, :, 3])
    h = mlstm_scan(qh, kh, vh, ig_f, lf_f) + flip(mlstm_scan(flip(qh), flip(kh), flip(vh), flip(ig_b), flip(lf_b)))
    return head_rmsnorm(h, g_norm) * jax.nn.sigmoid(o.astype(f32))


def encoder_layer(x, g_ffn1, w_ffn1_in, w_ffn1_out, g_mix, w_in, gla_w2_fwd, gla_b2_fwd, gla_w2_bwd, gla_b2_bwd,
                  gla_norm, ml_conv_w, ml_conv_b, ml_b_if, ml_norm, w_out, g_ffn2, w_ffn2_in, w_ffn2_out):
    f32 = jnp.float32
    x = x + 0.5 * swiglu(rmsnorm(x, g_ffn1), w_ffn1_in, w_ffn1_out)
    h = rmsnorm(x, g_mix)
    (gla_q, gla_k, gla_v, gla_r, gla_lr, ml_qk, ml_v, ml_o, ml_if, gate_a, gate_b) = jnp.split(h @ w_in, IN_OFFSETS, axis=-1)
    o_a = gla_mixer(gla_q, gla_k, gla_v, gla_r, gla_lr, gla_w2_fwd, gla_b2_fwd, gla_w2_bwd, gla_b2_bwd, gla_norm)
    o_b = mlstm_mixer(ml_qk, ml_v, ml_o, ml_if, ml_conv_w, ml_conv_b, ml_b_if, ml_norm)
    merged = jax.nn.sigmoid(gate_a.astype(f32)) * o_a + jax.nn.sigmoid(gate_b.astype(f32)) * o_b
    x = x + merged.astype(x.dtype) @ w_out
    x = x + 0.5 * swiglu(rmsnorm(x, g_ffn2), w_ffn2_in, w_ffn2_out)
    return x


def setup_inputs(seed: int = 0) -> dict:
    key = jax.random.key(seed)
    ks = jax.random.split(key, 24)
    f32 = jnp.float32

    def nrm(k, shape, scale):
        return jax.random.normal(k, shape, f32) * scale

    def gain(k, shape):
        return 1.0 + 0.02 * jax.random.normal(k, shape, f32)

    forget_bias = jnp.concatenate([jnp.zeros((N_HEADS_ML,), f32), jnp.linspace(3.0, 6.0, N_HEADS_ML, dtype=f32)])
    if_base = jnp.tile(forget_bias, 2)
    return {
        "x_prompt": jax.random.normal(ks[0], (BATCH, SEQ, D_MODEL), f32),
        "x_sample": jax.random.normal(ks[1], (DEC_BATCH, DEC_SEQ, D_MODEL), f32),
        "g_ffn1": gain(ks[2], (DEPTH, D_MODEL)),
        "w_ffn1_in": nrm(ks[3], (DEPTH, D_MODEL, 2 * D_FF), D_MODEL ** -0.5),
        "w_ffn1_out": nrm(ks[4], (DEPTH, D_FF, D_MODEL), D_FF ** -0.5),
        "g_mix": gain(ks[5], (DEPTH, D_MODEL)),
        "w_in": nrm(ks[6], (DEPTH, D_MODEL, D_IN), D_MODEL ** -0.5),
        "gla_w2_fwd": nrm(ks[7], (DEPTH, GLA_LOWRANK, GLA_QK), GLA_LOWRANK ** -0.5),
        "gla_b2_fwd": nrm(ks[8], (DEPTH, GLA_QK), 0.1),
        "gla_w2_bwd": nrm(ks[9], (DEPTH, GLA_LOWRANK, GLA_QK), GLA_LOWRANK ** -0.5),
        "gla_b2_bwd": nrm(ks[10], (DEPTH, GLA_QK), 0.1),
        "gla_norm": gain(ks[11], (DEPTH, GLA_V)),
        "ml_conv_w": nrm(ks[12], (DEPTH, CONV_W, 2 * ML_QK), CONV_W ** -0.5),
        "ml_conv_b": nrm(ks[13], (DEPTH, 2 * ML_QK), 0.02),
        "ml_b_if": if_base + nrm(ks[14], (DEPTH, 4 * N_HEADS_ML), 0.1),
        "ml_norm": gain(ks[15], (DEPTH, ML_V)),
        "w_out": nrm(ks[16], (DEPTH, D_MODEL, D_MODEL), D_MODEL ** -0.5),
        "g_ffn2": gain(ks[17], (DEPTH, D_MODEL)),
        "w_ffn2_in": nrm(ks[18], (DEPTH, D_MODEL, 2 * D_FF), D_MODEL ** -0.5),
        "w_ffn2_out": nrm(ks[19], (DEPTH, D_FF, D_MODEL), D_FF ** -0.5),
        "g_final": gain(ks[20], (D_MODEL,)),
    }


def reference(x_prompt, x_sample, g_ffn1, w_ffn1_in, w_ffn1_out, g_mix, w_in, gla_w2_fwd, gla_b2_fwd,
              gla_w2_bwd, gla_b2_bwd, gla_norm, ml_conv_w, ml_conv_b, ml_b_if, ml_norm, w_out, g_ffn2,
              w_ffn2_in, w_ffn2_out, g_final):
    def trunk(x):
        for l in range(DEPTH):
            x = encoder_layer(x, g_ffn1[l], w_ffn1_in[l], w_ffn1_out[l], g_mix[l], w_in[l],
                              gla_w2_fwd[l], gla_b2_fwd[l], gla_w2_bwd[l], gla_b2_bwd[l], gla_norm[l],
                              ml_conv_w[l], ml_conv_b[l], ml_b_if[l], ml_norm[l], w_out[l],
                              g_ffn2[l], w_ffn2_in[l], w_ffn2_out[l])
        return rmsnorm(x, g_final)

    y_prompt = trunk(x_prompt)
    y_sample = trunk(x_sample)
    return (y_prompt, y_sample)
```

```python
import functools

import jax
import jax.numpy as jnp
from jax import lax
from jax.experimental import pallas as pl
from jax.experimental.pallas import tpu as pltpu

F32 = jnp.float32
BF16 = jnp.bfloat16

D_MODEL = 1024
D_FF = 2816
N_HEADS = 4
DK = 128
DV = 256
GLA_LOWRANK = 16
GLA_TAU = 16.0
EPS = 1e-6
QK_SCALE = DK ** -0.5

LANES = 128
CHUNK = 128
GLA_MID = CHUNK // 2
TM_FFN = 512
TF = 256
TN_IN = 512
VMEM_LIMIT = 56 * 1024 * 1024

N_MAIN = 7168
OFF_GQ, OFF_GV, OFF_GR, OFF_MQ, OFF_MV, OFF_MO, OFF_GA, OFF_GB = 0, 512, 1536, 2560, 3072, 4096, 5120, 6144
SM_LR_F, SM_LR_B, SM_IF = 0, GLA_LOWRANK, 2 * GLA_LOWRANK


def _dot(a, b):
    return jnp.dot(a, b, preferred_element_type=F32)


def _dot_nt(a, b):
    return lax.dot_general(a, b, (((1,), (1,)), ((), ())), preferred_element_type=F32)


def _rms(x, g):
    return x * lax.rsqrt(jnp.mean(x * x, axis=-1, keepdims=True) + EPS) * g


def _logsig(z):
    return jnp.minimum(z, 0.0) - jnp.log1p(jnp.exp(-jnp.abs(z)))


def _split_hi_lo(x):
    hi = x.astype(BF16)
    lo = (x - hi.astype(F32)).astype(BF16)
    return hi, lo


def _sel_dot(sel_bf16, x):
    hi, lo = _split_hi_lo(x)
    return _dot(sel_bf16, hi) + _dot(sel_bf16, lo)


def _dot_sel(x, sel_bf16):
    hi, lo = _split_hi_lo(x)
    return _dot(hi, sel_bf16) + _dot(lo, sel_bf16)


def _swiglu_into(h_scr, win_ref, wout_ref, act_scr):
    for j in range(D_FF // TF):
        a = _dot(h_scr[...], win_ref[:, j * TF:(j + 1) * TF])
        g = _dot(h_scr[...], win_ref[:, D_FF + j * TF:D_FF + (j + 1) * TF])
        act_scr[:, j * TF:(j + 1) * TF] = (a * jax.nn.sigmoid(a) * g).astype(BF16)
    return _dot(act_scr[...], wout_ref[...])


def _ffn1_kernel(x_ref, g_ref, win_ref, wout_ref, o_ref, h_scr, act_scr):
    x = x_ref[...]
    h_scr[...] = _rms(x, g_ref[...]).astype(BF16)
    o_ref[...] = x + 0.5 * _swiglu_into(h_scr, win_ref, wout_ref, act_scr)


def _outffn_kernel(x_ref, m_ref, wo_ref, g_ref, win_ref, wout_ref, gf_ref, o_ref, h_scr, act_scr, *, final_norm):
    x = x_ref[...] + _dot(m_ref[...], wo_ref[...])
    h_scr[...] = _rms(x, g_ref[...]).astype(BF16)
    x = x + 0.5 * _swiglu_into(h_scr, win_ref, wout_ref, act_scr)
    o_ref[...] = _rms(x, gf_ref[...]) if final_norm else x


def _inproj_kernel(x_ref, g_ref, wm_ref, wkt_ref, ws_ref, wst_ref, um_ref, kt_ref, sm_ref, smt_ref, h_scr):
    tm = x_ref.shape[1]
    h_scr[...] = _rms(x_ref[0], g_ref[...]).astype(BF16)
    for j in range(N_MAIN // TN_IN):
        cols = slice(j * TN_IN, (j + 1) * TN_IN)
        um_ref[0, :, cols] = _dot(h_scr[...], wm_ref[:, cols]).astype(BF16)
    kt = _dot_nt(wkt_ref[...], h_scr[...])
    sm_ref[0] = _dot(h_scr[...], ws_ref[...])
    smt = _dot_nt(wst_ref[...], h_scr[...])
    for j in range(tm // CHUNK):
        kt_ref[0, j] = kt[:, j * CHUNK:(j + 1) * CHUNK].astype(BF16)
        smt_ref[0, j] = smt[:, j * CHUNK:(j + 1) * CHUNK]


def _mixer_kernel(gq_ref, gv_ref, gr_ref, mq_ref, mv_ref, mo_ref, ga_ref, gb_ref, gkt_ref, mkt_ref, sm_ref, smt_ref,
                  w2_ref, b2_ref, w2t_ref, b2t_ref, selrow_ref, bifrow_ref, selcol_ref, bifcol_ref,
                  cwq_ref, cbq_ref, cwkt_ref, cbkt_ref, gn_ref, mn_ref,
                  out_ref,
                  qc_scr, kct_scr, og_scr, hm_scr, s_scr, c_scr):
    seq = gq_ref.shape[1]
    nch = seq // CHUNK
    ri = lax.broadcasted_iota(jnp.int32, (CHUNK, CHUNK), 0)
    ci = lax.broadcasted_iota(jnp.int32, (CHUNK, CHUNK), 1)
    lower = ci <= ri
    upper = ci >= ri
    lower_bf = lower.astype(BF16)
    upper_bf = upper.astype(BF16)
    ones_col = (lax.broadcasted_iota(jnp.int32, (CHUNK, LANES), 1) == 0).astype(BF16)

    xq = mq_ref[0].astype(F32)
    rid = lax.broadcasted_iota(jnp.int32, (seq, DK), 0)
    prev = jnp.where(rid == 0, 0.0, pltpu.roll(xq, 1, 0))
    nxt = jnp.where(rid == seq - 1, 0.0, pltpu.roll(xq, seq - 1, 0))
    cw = cwq_ref[0]
    conv = cbq_ref[0] + prev * cw[0:1] + xq * cw[1:2] + nxt * cw[2:3]
    qc_scr[...] = conv * jax.nn.sigmoid(conv) * QK_SCALE

    lane = lax.broadcasted_iota(jnp.int32, (DK, CHUNK), 1)
    cwk = cwkt_ref[0]
    cbk = cbkt_ref[0]

    def kconv_body(c, carry):
        x = mkt_ref[0, c].astype(F32)
        xp = mkt_ref[0, jnp.maximum(c - 1, 0)].astype(F32) * jnp.where(c > 0, 1.0, 0.0)
        xn = mkt_ref[0, jnp.minimum(c + 1, nch - 1)].astype(F32) * jnp.where(c < nch - 1, 1.0, 0.0)
        prv = jnp.where(lane == 0, pltpu.roll(xp, 1, 1), pltpu.roll(x, 1, 1))
        nx = jnp.where(lane == CHUNK - 1, pltpu.roll(xn, CHUNK - 1, 1), pltpu.roll(x, CHUNK - 1, 1))
        cv = cbk + prv * cwk[:, 0:1] + x * cwk[:, 1:2] + nx * cwk[:, 2:3]
        kct_scr[c] = cv * jax.nn.sigmoid(cv)
        return carry

    lax.fori_loop(0, nch, kconv_body, 0)

    def gla_chunk(c, rows, d, cum, cum_t, mask, r_idx, l_idx):
        sm = sm_ref[0, rows, :].astype(BF16)
        la = _logsig(_dot(sm, w2_ref[0, d]) + b2_ref[0, d]) * (1.0 / GLA_TAU)
        b = _sel_dot(cum, la)
        smt = smt_ref[0, c].astype(BF16)
        lat = _logsig(_dot(w2t_ref[0, d], smt) + b2t_ref[0, d]) * (1.0 / GLA_TAU)
        bt = _dot_sel(lat, cum_t)
        rho = b[r_idx:r_idx + 1, :]
        rho_t = bt[:, r_idx:r_idx + 1]
        bl_t = bt[:, l_idx:l_idx + 1]
        q = gq_ref[0, rows, :].astype(F32) * QK_SCALE
        q_mid = (q * jnp.exp(b - rho)).astype(BF16)
        q_in = (q * jnp.exp(b)).astype(BF16)
        kt = gkt_ref[0, c].astype(F32)
        k_mid = (kt * jnp.exp(rho_t - bt)).astype(BF16)
        k_st = (kt * jnp.exp(bl_t - bt)).astype(BF16)
        v = gv_ref[0, rows, :]
        att = jnp.where(mask, _dot(q_mid, k_mid), 0.0).astype(BF16)
        s = s_scr[...]
        o = _dot(jnp.concatenate([att, q_in], axis=1), jnp.concatenate([v, s.astype(BF16)], axis=0))
        s_scr[...] = s * jnp.exp(bl_t) + _dot(k_st, v)
        return o

    def ml_chunk(c, rows, gi, cum, cum_t, mask, l_idx, m_prev):
        g_t = _sel_dot(selrow_ref[0], smt_ref[0, c]) + bifrow_ref[0]
        cum_row = _dot_sel(_logsig(g_t), cum_t)
        i_row = g_t[gi:gi + 1, :]
        b_row = cum_row[gi + 1:gi + 2, :]
        g_c = _dot_sel(sm_ref[0, rows, :], selcol_ref[0]) + bifcol_ref[0]
        cum_col = _sel_dot(cum, _logsig(g_c))
        b_col = cum_col[:, gi + 1:gi + 2]
        log_d = jnp.where(mask, b_col - b_row + i_row, -jnp.inf)
        m_inter = b_col + m_prev
        m_t = jnp.maximum(m_inter, jnp.max(log_d, axis=1, keepdims=True))
        d_m = jnp.exp(log_d - m_t)
        w_inter = jnp.exp(m_inter - m_t)
        q = qc_scr[rows, :]
        kt = kct_scr[c]
        s = _dot(q.astype(BF16), kt.astype(BF16)) * d_m
        v_aug = jnp.concatenate([mv_ref[0, rows, :], ones_col], axis=1)
        cst = c_scr[...]
        out = _dot(jnp.concatenate([s.astype(BF16), (w_inter * q).astype(BF16)], axis=1),
                   jnp.concatenate([v_aug, cst.astype(BF16)], axis=0))
        num = out[:, :DV]
        den = out[:, DV:DV + 1]
        h = num * (1.0 / jnp.maximum(jnp.abs(den), jnp.exp(-m_t)))
        b_last = b_row[:, l_idx:l_idx + 1]
        log_w = b_last - b_row + i_row
        m_new = jnp.maximum(b_last + m_prev, jnp.max(log_w, axis=1, keepdims=True))
        w_st = jnp.exp(log_w - m_new)
        decay = jnp.exp(b_last + m_prev - m_new)
        c_scr[...] = decay * cst + _dot((kt * w_st).astype(BF16), v_aug)
        return h, m_new

    m0 = jnp.full((1, 1), -jnp.inf, F32)

    s_scr[...] = jnp.zeros_like(s_scr)
    c_scr[...] = jnp.zeros_like(c_scr)

    def fwd_body(c, m_prev):
        rows = pl.ds(pl.multiple_of(c * CHUNK, CHUNK), CHUNK)
        og_scr[rows, :] = gla_chunk(c, rows, 0, lower_bf, upper_bf, lower, GLA_MID - 1, CHUNK - 1)
        h, m_new = ml_chunk(c, rows, 0, lower_bf, upper_bf, lower, CHUNK - 1, m_prev)
        hm_scr[rows, :] = h
        return m_new

    lax.fori_loop(0, nch, fwd_body, m0)

    s_scr[...] = jnp.zeros_like(s_scr)
    c_scr[...] = jnp.zeros_like(c_scr)

    def bwd_body(i, m_prev):
        c = nch - 1 - i
        rows = pl.ds(pl.multiple_of(c * CHUNK, CHUNK), CHUNK)
        og = og_scr[rows, :] + gla_chunk(c, rows, 1, upper_bf, lower_bf, upper, GLA_MID, 0)
        h, m_new = ml_chunk(c, rows, 2, upper_bf, lower_bf, upper, 0, m_prev)
        hm = hm_scr[rows, :] + h
        r = gr_ref[0, rows, :].astype(F32)
        o_a = (og * lax.rsqrt(jnp.mean(og * og, axis=-1, keepdims=True) + EPS) * gn_ref[0]) * (r * jax.nn.sigmoid(r))
        o_b = (hm * lax.rsqrt(jnp.mean(hm * hm, axis=-1, keepdims=True) + EPS) * mn_ref[0]) \
            * jax.nn.sigmoid(mo_ref[0, rows, :].astype(F32))
        merged = jax.nn.sigmoid(ga_ref[0, rows, :].astype(F32)) * o_a \
            + jax.nn.sigmoid(gb_ref[0, rows, :].astype(F32)) * o_b
        out_ref[0, rows, :] = merged.astype(BF16)
        return m_new

    lax.fori_loop(0, nch, bwd_body, m0)


def _resident(shape):
    nd = len(shape)
    return pl.BlockSpec(shape, lambda *_: (0,) * nd, pipeline_mode=pl.Buffered(1))


def _params(n_axes):
    return pltpu.CompilerParams(dimension_semantics=("arbitrary",) * n_axes, vmem_limit_bytes=VMEM_LIMIT)


def _ffn1(x2d, g, win, wout):
    m = x2d.shape[0]
    tile = pl.BlockSpec((TM_FFN, D_MODEL), lambda i: (i, 0))
    return pl.pallas_call(
        _ffn1_kernel,
        grid=(m // TM_FFN,),
        in_specs=[tile, _resident(g.shape), _resident(win.shape), _resident(wout.shape)],
        out_specs=tile,
        out_shape=jax.ShapeDtypeStruct((m, D_MODEL), F32),
        scratch_shapes=[pltpu.VMEM((TM_FFN, D_MODEL), BF16), pltpu.VMEM((TM_FFN, D_FF), BF16)],
        compiler_params=_params(1),
        name="ffn1",
    )(x2d, g, win, wout)


def _outffn(x2d, merged2d, wo, g, win, wout, gf, final_norm):
    m = x2d.shape[0]
    tile = pl.BlockSpec((TM_FFN, D_MODEL), lambda i: (i, 0))
    return pl.pallas_call(
        functools.partial(_outffn_kernel, final_norm=final_norm),
        grid=(m // TM_FFN,),
        in_specs=[tile, tile, _resident(wo.shape), _resident(g.shape), _resident(win.shape), _resident(wout.shape),
                  _resident(gf.shape)],
        out_specs=tile,
        out_shape=jax.ShapeDtypeStruct((m, D_MODEL), F32),
        scratch_shapes=[pltpu.VMEM((TM_FFN, D_MODEL), BF16), pltpu.VMEM((TM_FFN, D_FF), BF16)],
        compiler_params=_params(1),
        name="outffn",
    )(x2d, merged2d, wo, g, win, wout, gf)


def _inproj(x, g, wm, wkt, ws, wst):
    bsz, seq, _ = x.shape
    tm = TM_FFN
    nkt = wkt.shape[0]
    return pl.pallas_call(
        _inproj_kernel,
        grid=(bsz, seq // tm),
        in_specs=[pl.BlockSpec((1, tm, D_MODEL), lambda b, i: (b, i, 0)),
                  _resident(g.shape), _resident(wm.shape), _resident(wkt.shape), _resident(ws.shape),
                  _resident(wst.shape)],
        out_specs=[pl.BlockSpec((1, tm, N_MAIN), lambda b, i: (b, i, 0)),
                   pl.BlockSpec((1, tm // CHUNK, nkt, CHUNK), lambda b, i: (b, i, 0, 0)),
                   pl.BlockSpec((1, tm, LANES), lambda b, i: (b, i, 0)),
                   pl.BlockSpec((1, tm // CHUNK, LANES, CHUNK), lambda b, i: (b, i, 0, 0))],
        out_shape=[jax.ShapeDtypeStruct((bsz, seq, N_MAIN), BF16),
                   jax.ShapeDtypeStruct((bsz, seq // CHUNK, nkt, CHUNK), BF16),
                   jax.ShapeDtypeStruct((bsz, seq, LANES), F32),
                   jax.ShapeDtypeStruct((bsz, seq // CHUNK, LANES, CHUNK), F32)],
        scratch_shapes=[pltpu.VMEM((tm, D_MODEL), BF16)],
        compiler_params=_params(2),
        name="inproj",
    )(x, g, wm, wkt, ws, wst)


def _mixer(um, kt, sm, smt, mp):
    bsz, seq, _ = um.shape
    nch = seq // CHUNK

    def col(width, off):
        base = off // width
        return pl.BlockSpec((1, seq, width), lambda b, h: (b, 0, base + h))

    def ktspec(base):
        return pl.BlockSpec((1, nch, DK, CHUNK), lambda b, h: (b, 0, base + h, 0))

    def head(arr):
        shp = arr.shape
        nd = len(shp)
        return pl.BlockSpec((1,) + shp[1:], lambda b, h: (h,) + (0,) * (nd - 1))

    plist = [mp["w2"], mp["b2"], mp["w2t"], mp["b2t"], mp["selrow"], mp["bifrow"], mp["selcol"], mp["bifcol"],
             mp["cwq"], mp["cbq"], mp["cwkt"], mp["cbkt"], mp["gn"], mp["mn"]]
    in_specs = [col(DK, OFF_GQ), col(DV, OFF_GV), col(DV, OFF_GR), col(DK, OFF_MQ), col(DV, OFF_MV),
                col(DV, OFF_MO), col(DV, OFF_GA), col(DV, OFF_GB),
                ktspec(0), ktspec(N_HEADS),
                pl.BlockSpec((1, seq, LANES), lambda b, h: (b, 0, 0)),
                pl.BlockSpec((1, nch, LANES, CHUNK), lambda b, h: (b, 0, 0, 0))] + [head(p) for p in plist]
    return pl.pallas_call(
        _mixer_kernel,
        grid=(bsz, N_HEADS),
        in_specs=in_specs,
        out_specs=pl.BlockSpec((1, seq, DV), lambda b, h: (b, 0, h)),
        out_shape=jax.ShapeDtypeStruct((bsz, seq, N_HEADS * DV), BF16),
        scratch_shapes=[pltpu.VMEM((seq, DK), F32), pltpu.VMEM((nch, DK, CHUNK), F32),
                        pltpu.VMEM((seq, DV), F32), pltpu.VMEM((seq, DV), F32),
                        pltpu.VMEM((DK, DV), F32), pltpu.VMEM((DK, DV + LANES), F32)],
        compiler_params=_params(2),
        name="mixer",
    )(um, um, um, um, um, um, um, um, kt, kt, sm, smt, *plist)


def _prep_layer(w_in, gla_w2_fwd, gla_b2_fwd, gla_w2_bwd, gla_b2_bwd, gla_norm, ml_conv_w, ml_conv_b, ml_b_if,
                ml_norm):
    sizes = (512, 512, 1024, 1024, 32, 1024, 1024, 1024, 16, 1024, 1024)
    offs = [0]
    for s in sizes:
        offs.append(offs[-1] + s)
    part = [w_in[:, offs[i]:offs[i + 1]] for i in range(len(sizes))]
    gq, gk, gv, gr, glr, mqk, mv, mo, mif, gate_a, gate_b = part
    mq, mk = mqk[:, :512], mqk[:, 512:]
    wm = jnp.concatenate([gq, gv, gr, mq, mv, mo, gate_a, gate_b], axis=1).astype(BF16)
    wkt = jnp.concatenate([gk, mk], axis=1).T.astype(BF16)
    ws = jnp.concatenate([glr, mif, jnp.zeros((D_MODEL, LANES - 48), F32)], axis=1).astype(BF16)
    wst = ws.T

    hd = lambda a: a.reshape(a.shape[:-1] + (N_HEADS, DK))
    def w2pad(w2, row0):
        w = jnp.moveaxis(hd(w2), 1, 0)
        return jnp.pad(w, ((0, 0), (row0, LANES - row0 - GLA_LOWRANK), (0, 0)))
    w2 = jnp.stack([w2pad(gla_w2_fwd, SM_LR_F), w2pad(gla_w2_bwd, SM_LR_B)], axis=1).astype(BF16)
    w2t = jnp.swapaxes(w2, 2, 3)
    b2 = jnp.stack([hd(gla_b2_fwd), hd(gla_b2_bwd)], axis=1)[:, :, None, :]
    b2t = jnp.swapaxes(b2, 2, 3)

    g_idx = jnp.arange(4)
    h_idx = jnp.arange(N_HEADS)
    src = SM_IF + 4 * g_idx[None, :] + h_idx[:, None]
    selrow = jnp.zeros((N_HEADS, 8, LANES), F32).at[h_idx[:, None], g_idx[None, :], src].set(1.0)
    selcol = jnp.swapaxes(jnp.zeros((N_HEADS, LANES, LANES), F32).at[h_idx[:, None], g_idx[None, :], src].set(1.0),
                          1, 2)
    bif = ml_b_if.reshape(4, N_HEADS).T
    bifrow = jnp.pad(bif, ((0, 0), (0, 4)))[:, :, None]
    bifcol = jnp.pad(bif, ((0, 0), (0, LANES - 4)))[:, None, :]

    cw_q = jnp.moveaxis(hd(ml_conv_w[:, :512]), 1, 0)
    cw_k = jnp.moveaxis(hd(ml_conv_w[:, 512:]), 1, 0)
    cb_q = hd(ml_conv_b[:512])[:, None, :]
    cb_k = hd(ml_conv_b[512:])[:, :, None]
    mp = dict(w2=w2, b2=b2, w2t=w2t, b2t=b2t, selrow=selrow.astype(BF16), bifrow=bifrow,
              selcol=selcol.astype(BF16), bifcol=bifcol, cwq=cw_q, cbq=cb_q, cwkt=jnp.swapaxes(cw_k, 1, 2),
              cbkt=cb_k, gn=gla_norm.reshape(N_HEADS, 1, DV), mn=ml_norm.reshape(N_HEADS, 1, DV))
    return wm, wkt, ws, wst, mp


def kernel(x_prompt, x_sample, g_ffn1, w_ffn1_in, w_ffn1_out, g_mix, w_in, gla_w2_fwd, gla_b2_fwd, gla_w2_bwd,
           gla_b2_bwd, gla_norm, ml_conv_w, ml_conv_b, ml_b_if, ml_norm, w_out, g_ffn2, w_ffn2_in, w_ffn2_out,
           g_final):
    depth = w_in.shape[0]
    layers = []
    for l in range(depth):
        layers.append(dict(
            g1=g_ffn1[l][None, :], w1i=w_ffn1_in[l].astype(BF16), w1o=w_ffn1_out[l].astype(BF16),
            gm=g_mix[l][None, :],
            mix=_prep_layer(w_in[l], gla_w2_fwd[l], gla_b2_fwd[l], gla_w2_bwd[l], gla_b2_bwd[l], gla_norm[l],
                            ml_conv_w[l], ml_conv_b[l], ml_b_if[l], ml_norm[l]),
            wo=w_out[l].astype(BF16), g2=g_ffn2[l][None, :], w2i=w_ffn2_in[l].astype(BF16),
            w2o=w_ffn2_out[l].astype(BF16)))
    gf = g_final[None, :]

    def trunk(x):
        bsz, seq, _ = x.shape
        for l, p in enumerate(layers):
            wm, wkt, ws, wst, mp = p["mix"]
            x1 = _ffn1(x.reshape(bsz * seq, D_MODEL), p["g1"], p["w1i"], p["w1o"])
            um, kt, sm, smt = _inproj(x1.reshape(bsz, seq, D_MODEL), p["gm"], wm, wkt, ws, wst)
            merged = _mixer(um, kt, sm, smt, mp)
            x = _outffn(x1, merged.reshape(bsz * seq, D_MODEL), p["wo"], p["g2"], p["w2i"], p["w2o"], gf,
                        final_norm=(l == depth - 1))
            x = x.reshape(bsz, seq, D_MODEL)
        return x

    return trunk(x_prompt), trunk(x_sample)
```

```python
import functools

import jax
import jax.numpy as jnp
from jax import lax
from jax.experimental import pallas as pl
from jax.experimental.pallas import tpu as pltpu

F32 = jnp.float32
BF16 = jnp.bfloat16

D_MODEL = 1024
D_FF = 2816
N_HEADS = 4
DK = 128
DV = 256
GLA_LOWRANK = 16
GLA_TAU = 16.0
EPS = 1e-6
QK_SCALE = DK ** -0.5

LANES = 128
CHUNK = 128
GLA_MID = CHUNK // 2
TM_FFN = 512
TF = 256
TN_IN = 512
GATE_UNROLL = 2
VMEM_LIMIT = 56 * 1024 * 1024

GROUPS = ("gqk", "gv", "gr", "mqk", "mv", "mo", "ga", "gb")
N_MAIN = len(GROUPS) * N_HEADS * DV
SM_LR_F, SM_LR_B, SM_IF = 0, GLA_LOWRANK, 2 * GLA_LOWRANK


def _dot(a, b):
    return jnp.dot(a, b, preferred_element_type=F32)


def _dot_nt(a, b):
    return lax.dot_general(a, b, (((1,), (1,)), ((), ())), preferred_element_type=F32)


def _dot_tn(a, b):
    return lax.dot_general(a, b, (((0,), (0,)), ((), ())), preferred_element_type=F32)


def _rms(x, g):
    return x * lax.rsqrt(jnp.mean(x * x, axis=-1, keepdims=True) + EPS) * g


def _logsig(z):
    return jnp.minimum(z, 0.0) - jnp.log1p(jnp.exp(-jnp.abs(z)))


def _split_hi_lo(x):
    hi = x.astype(BF16)
    lo = (x - hi.astype(F32)).astype(BF16)
    return hi, lo


def _sel_dot(sel_bf16, x):
    hi, lo = _split_hi_lo(x)
    return _dot(sel_bf16, hi) + _dot(sel_bf16, lo)


def _dot_sel(x, sel_bf16):
    hi, lo = _split_hi_lo(x)
    return _dot(hi, sel_bf16) + _dot(lo, sel_bf16)


def _swiglu_into(h_scr, win_ref, wout_ref, act_scr):
    for j in range(D_FF // TF):
        a = _dot(h_scr[...], win_ref[:, j * TF:(j + 1) * TF])
        g = _dot(h_scr[...], win_ref[:, D_FF + j * TF:D_FF + (j + 1) * TF])
        act_scr[:, j * TF:(j + 1) * TF] = (a * jax.nn.sigmoid(a) * g).astype(BF16)
    return _dot(act_scr[...], wout_ref[...])


def _ffn1_kernel(x_ref, g_ref, win_ref, wout_ref, o_ref, h_scr, act_scr):
    x = x_ref[...]
    h_scr[...] = _rms(x, g_ref[...]).astype(BF16)
    o_ref[...] = x + 0.5 * _swiglu_into(h_scr, win_ref, wout_ref, act_scr)


def _outffn_kernel(x_ref, m_ref, wo_ref, g_ref, win_ref, wout_ref, gf_ref, o_ref, h_scr, act_scr, *, final_norm):
    x = x_ref[...] + _dot(m_ref[...], wo_ref[...])
    h_scr[...] = _rms(x, g_ref[...]).astype(BF16)
    x = x + 0.5 * _swiglu_into(h_scr, win_ref, wout_ref, act_scr)
    o_ref[...] = _rms(x, gf_ref[...]) if final_norm else x


def _inproj_kernel(x_ref, g_ref, wm_ref, ws_ref, um_ref, sm_ref, h_scr):
    h_scr[...] = _rms(x_ref[0], g_ref[...]).astype(BF16)
    for j in range(N_MAIN // TN_IN):
        cols = slice(j * TN_IN, (j + 1) * TN_IN)
        um_ref[0, :, cols] = _dot(h_scr[...], wm_ref[:, cols]).astype(BF16)
    sm_ref[0] = _dot(h_scr[...], ws_ref[...])


def _mixer_kernel(gqk_ref, gv_ref, gr_ref, mqk_ref, mv_ref, mo_ref, ga_ref, gb_ref, sm_ref,
                  w2_ref, b2_ref, selcol_ref, bifcol_ref, cw_ref, cb_ref, gn_ref, mn_ref,
                  out_ref,
                  mqk_scr, qmid_scr, qin_scr, kmid_scr, kst_scr, gcol_scr, dec_scr, og_scr, hm_scr, s_scr, c_scr):
    seq = gqk_ref.shape[1]
    nch = seq // CHUNK
    ri = lax.broadcasted_iota(jnp.int32, (CHUNK, CHUNK), 0)
    ci = lax.broadcasted_iota(jnp.int32, (CHUNK, CHUNK), 1)
    lower = ci <= ri
    upper = ci >= ri
    eye = ci == ri
    lower_bf = lower.astype(BF16)
    upper_bf = upper.astype(BF16)
    ones_col = (lax.broadcasted_iota(jnp.int32, (CHUNK, LANES), 1) == 0).astype(BF16)

    def chunk_rows(c):
        return pl.ds(pl.multiple_of(c * CHUNK, CHUNK), CHUNK)

    x = mqk_ref[0].astype(F32)
    rid = lax.broadcasted_iota(jnp.int32, (seq, 2 * DK), 0)
    cid = lax.broadcasted_iota(jnp.int32, (1, 2 * DK), 1)
    prev = jnp.where(rid == 0, 0.0, pltpu.roll(x, 1, 0))
    nxt = jnp.where(rid == seq - 1, 0.0, pltpu.roll(x, seq - 1, 0))
    cw = cw_ref[0]
    conv = cb_ref[0] + prev * cw[0:1] + x * cw[1:2] + nxt * cw[2:3]
    mqk_scr[...] = conv * jax.nn.sigmoid(conv) * jnp.where(cid < DK, QK_SCALE, 1.0)

    def gate_body(c, carry):
        rows = chunk_rows(c)
        sm = sm_ref[0, rows, :]
        la = _logsig(_dot(sm.astype(BF16), w2_ref[0]) + b2_ref[0]) * (1.0 / GLA_TAU)
        g_c = _dot_sel(sm, selcol_ref[0]) + bifcol_ref[0]
        lf = _logsig(g_c)
        pre = _sel_dot(lower_bf, jnp.concatenate([la[:, :DK], lf], axis=1))
        suf = _sel_dot(upper_bf, jnp.concatenate([la[:, DK:], lf], axis=1))
        lane = lax.broadcasted_iota(jnp.int32, (CHUNK, LANES), 1)
        gcol_scr[rows, :] = jnp.where(lane == 1, pre[:, DK:], jnp.where(lane == 3, suf[:, DK:], g_c))
        qk = gqk_ref[0, rows, :].astype(F32)
        q = qk[:, :DK] * QK_SCALE
        k = qk[:, DK:]
        for d, b, r_idx, l_idx in ((0, pre[:, :DK], GLA_MID - 1, CHUNK - 1), (1, suf[:, :DK], GLA_MID, 0)):
            rho = b[r_idx:r_idx + 1, :]
            b_last = b[l_idx:l_idx + 1, :]
            qmid_scr[d, rows, :] = (q * jnp.exp(b - rho)).astype(BF16)
            qin_scr[d, rows, :] = (q * jnp.exp(b)).astype(BF16)
            kmid_scr[d, rows, :] = (k * jnp.exp(rho - b)).astype(BF16)
            kst_scr[d, rows, :] = (k * jnp.exp(b_last - b)).astype(BF16)
            dec_scr[d, c] = jnp.broadcast_to(jnp.exp(b_last), (8, DK))
        return carry

    lax.fori_loop(0, nch, gate_body, 0, unroll=GATE_UNROLL)

    def gla_step(d, c, rows, mask):
        att = jnp.where(mask, _dot_nt(qmid_scr[d, rows, :], kmid_scr[d, rows, :]), 0.0).astype(BF16)
        v = gv_ref[0, rows, :]
        s = s_scr[d]
        og_scr[d, rows, :] = _dot(jnp.concatenate([att, qin_scr[d, rows, :]], axis=1),
                                  jnp.concatenate([v, s.astype(BF16)], axis=0))
        dcol = jnp.sum(jnp.where(eye, dec_scr[d, c][0:1, :], 0.0), axis=1, keepdims=True)
        s_scr[d] = s * dcol + _dot_tn(kst_scr[d, rows, :], v)

    def ml_step(d, rows, mask, l_idx, m_prev):
        g = gcol_scr[rows, :]
        i_col = g[:, 2 * d:2 * d + 1]
        b_col = g[:, 2 * d + 1:2 * d + 2]
        c_col = i_col - b_col
        c_row = jnp.sum(jnp.where(eye, c_col, 0.0), axis=0, keepdims=True)
        log_d = jnp.where(mask, b_col + c_row, -jnp.inf)
        m_inter = b_col + m_prev
        m_t = jnp.maximum(m_inter, jnp.max(log_d, axis=1, keepdims=True))
        d_m = jnp.exp(log_d - m_t)
        w_inter = jnp.exp(m_inter - m_t)
        qk = mqk_scr[rows, :]
        q = qk[:, :DK]
        k = qk[:, DK:]
        s = _dot_nt(q.astype(BF16), k.astype(BF16)) * d_m
        v_aug = jnp.concatenate([mv_ref[0, rows, :], ones_col], axis=1)
        cst = c_scr[d]
        out = _dot(jnp.concatenate([s.astype(BF16), (w_inter * q).astype(BF16)], axis=1),
                   jnp.concatenate([v_aug, cst.astype(BF16)], axis=0))
        den = out[:, DV:DV + 1]
        hm_scr[d, rows, :] = out[:, :DV] * (1.0 / jnp.maximum(jnp.abs(den), jnp.exp(-m_t)))
        b_last = b_col[l_idx:l_idx + 1, :]
        log_w = b_last + c_col
        m_new = jnp.maximum(b_last + m_prev, jnp.max(log_w, axis=0, keepdims=True))
        w_st = jnp.exp(log_w - m_new)
        decay = jnp.exp(b_last + m_prev - m_new)
        c_scr[d] = decay * cst + _dot_tn((k * w_st).astype(BF16), v_aug)
        return m_new

    s_scr[...] = jnp.zeros_like(s_scr)
    c_scr[...] = jnp.zeros_like(c_scr)
    m0 = jnp.full((1, 1), -jnp.inf, F32)

    def scan_body(i, carry):
        m_f, m_b = carry
        cb = nch - 1 - i
        rows_f = chunk_rows(i)
        rows_b = chunk_rows(cb)
        gla_step(0, i, rows_f, lower)
        gla_step(1, cb, rows_b, upper)
        m_f = ml_step(0, rows_f, lower, CHUNK - 1, m_f)
        m_b = ml_step(1, rows_b, upper, 0, m_b)
        return m_f, m_b

    lax.fori_loop(0, nch, scan_body, (m0, m0))

    def merge_body(c, carry):
        rows = chunk_rows(c)
        og = og_scr[0, rows, :] + og_scr[1, rows, :]
        hm = hm_scr[0, rows, :] + hm_scr[1, rows, :]
        r = gr_ref[0, rows, :].astype(F32)
        o_a = (og * lax.rsqrt(jnp.mean(og * og, axis=-1, keepdims=True) + EPS) * gn_ref[0]) * (r * jax.nn.sigmoid(r))
        o_b = (hm * lax.rsqrt(jnp.mean(hm * hm, axis=-1, keepdims=True) + EPS) * mn_ref[0]) \
            * jax.nn.sigmoid(mo_ref[0, rows, :].astype(F32))
        merged = jax.nn.sigmoid(ga_ref[0, rows, :].astype(F32)) * o_a \
            + jax.nn.sigmoid(gb_ref[0, rows, :].astype(F32)) * o_b
        out_ref[0, rows, :] = merged.astype(BF16)
        return carry

    lax.fori_loop(0, nch, merge_body, 0, unroll=GATE_UNROLL)


def _resident(shape):
    nd = len(shape)
    return pl.BlockSpec(shape, lambda *_: (0,) * nd, pipeline_mode=pl.Buffered(1))


def _params(n_axes):
    return pltpu.CompilerParams(dimension_semantics=("arbitrary",) * n_axes, vmem_limit_bytes=VMEM_LIMIT)


def _ffn1(x2d, g, win, wout):
    m = x2d.shape[0]
    tile = pl.BlockSpec((TM_FFN, D_MODEL), lambda i: (i, 0))
    return pl.pallas_call(
        _ffn1_kernel,
        grid=(m // TM_FFN,),
        in_specs=[tile, _resident(g.shape), _resident(win.shape), _resident(wout.shape)],
        out_specs=tile,
        out_shape=jax.ShapeDtypeStruct((m, D_MODEL), F32),
        scratch_shapes=[pltpu.VMEM((TM_FFN, D_MODEL), BF16), pltpu.VMEM((TM_FFN, D_FF), BF16)],
        compiler_params=_params(1),
        name="ffn1",
    )(x2d, g, win, wout)


def _outffn(x2d, merged2d, wo, g, win, wout, gf, final_norm):
    m = x2d.shape[0]
    tile = pl.BlockSpec((TM_FFN, D_MODEL), lambda i: (i, 0))
    return pl.pallas_call(
        functools.partial(_outffn_kernel, final_norm=final_norm),
        grid=(m // TM_FFN,),
        in_specs=[tile, tile, _resident(wo.shape), _resident(g.shape), _resident(win.shape), _resident(wout.shape),
                  _resident(gf.shape)],
        out_specs=tile,
        out_shape=jax.ShapeDtypeStruct((m, D_MODEL), F32),
        scratch_shapes=[pltpu.VMEM((TM_FFN, D_MODEL), BF16), pltpu.VMEM((TM_FFN, D_FF), BF16)],
        compiler_params=_params(1),
        name="outffn",
    )(x2d, merged2d, wo, g, win, wout, gf)


def _inproj(x, g, wm, ws):
    bsz, seq, _ = x.shape
    tm = TM_FFN
    return pl.pallas_call(
        _inproj_kernel,
        grid=(bsz, seq // tm),
        in_specs=[pl.BlockSpec((1, tm, D_MODEL), lambda b, i: (b, i, 0)),
                  _resident(g.shape), _resident(wm.shape), _resident(ws.shape)],
        out_specs=[pl.BlockSpec((1, tm, N_MAIN), lambda b, i: (b, i, 0)),
                   pl.BlockSpec((1, tm, LANES), lambda b, i: (b, i, 0))],
        out_shape=[jax.ShapeDtypeStruct((bsz, seq, N_MAIN), BF16),
                   jax.ShapeDtypeStruct((bsz, seq, LANES), F32)],
        scratch_shapes=[pltpu.VMEM((tm, D_MODEL), BF16)],
        compiler_params=_params(2),
        name="inproj",
    )(x, g, wm, ws)


def _mixer(um, sm, mp):
    bsz, seq, _ = um.shape
    nch = seq // CHUNK

    def group(name):
        base = GROUPS.index(name) * N_HEADS
        return pl.BlockSpec((1, seq, DV), lambda b, h: (b, 0, base + h))

    def head(arr):
        nd = arr.ndim
        return pl.BlockSpec((1,) + arr.shape[1:], lambda b, h: (h,) + (0,) * (nd - 1))

    plist = [mp[n] for n in ("w2", "b2", "selcol", "bifcol", "cw", "cb", "gn", "mn")]
    in_specs = [group(n) for n in GROUPS] + [pl.BlockSpec((1, seq, LANES), lambda b, h: (b, 0, 0))] \
        + [head(p) for p in plist]
    return pl.pallas_call(
        _mixer_kernel,
        grid=(bsz, N_HEADS),
        in_specs=in_specs,
        out_specs=pl.BlockSpec((1, seq, DV), lambda b, h: (b, 0, h)),
        out_shape=jax.ShapeDtypeStruct((bsz, seq, N_HEADS * DV), BF16),
        scratch_shapes=[pltpu.VMEM((seq, 2 * DK), F32),
                        pltpu.VMEM((2, seq, DK), BF16), pltpu.VMEM((2, seq, DK), BF16),
                        pltpu.VMEM((2, seq, DK), BF16), pltpu.VMEM((2, seq, DK), BF16),
                        pltpu.VMEM((seq, LANES), F32), pltpu.VMEM((2, nch, 8, DK), F32),
                        pltpu.VMEM((2, seq, DV), F32), pltpu.VMEM((2, seq, DV), F32),
                        pltpu.VMEM((2, DK, DV), F32), pltpu.VMEM((2, DK, DV + LANES), F32)],
        compiler_params=_params(2),
        name="mixer",
    )(*([um] * len(GROUPS)), sm, *plist)


def _prep_layer(w_in, gla_w2_fwd, gla_b2_fwd, gla_w2_bwd, gla_b2_bwd, gla_norm, ml_conv_w, ml_conv_b, ml_b_if,
                ml_norm):
    sizes = (512, 512, 1024, 1024, 32, 1024, 1024, 1024, 16, 1024, 1024)
    offs = [0]
    for s in sizes:
        offs.append(offs[-1] + s)
    gq, gk, gv, gr, glr, mqk, mv, mo, mif, gate_a, gate_b = [w_in[:, offs[i]:offs[i + 1]] for i in range(len(sizes))]

    def pair(a, b):
        hd = lambda w: w.reshape(D_MODEL, N_HEADS, DK)
        return jnp.concatenate([hd(a), hd(b)], axis=2).reshape(D_MODEL, N_HEADS * 2 * DK)

    parts = dict(gqk=pair(gq, gk), gv=gv, gr=gr, mqk=pair(mqk[:, :512], mqk[:, 512:]), mv=mv, mo=mo, ga=gate_a,
                 gb=gate_b)
    wm = jnp.concatenate([parts[n] for n in GROUPS], axis=1).astype(BF16)
    ws = jnp.concatenate([glr, mif, jnp.zeros((D_MODEL, LANES - 48), F32)], axis=1).astype(BF16)

    hd = lambda a: a.reshape(a.shape[:-1] + (N_HEADS, DK))

    def w2pad(w2, row0):
        w = jnp.moveaxis(hd(w2), 1, 0)
        return jnp.pad(w, ((0, 0), (row0, LANES - row0 - GLA_LOWRANK), (0, 0)))

    w2 = jnp.concatenate([w2pad(gla_w2_fwd, SM_LR_F), w2pad(gla_w2_bwd, SM_LR_B)], axis=2).astype(BF16)
    b2 = jnp.concatenate([hd(gla_b2_fwd), hd(gla_b2_bwd)], axis=1)[:, None, :]

    g_idx = jnp.arange(4)
    h_idx = jnp.arange(N_HEADS)
    src = SM_IF + 4 * g_idx[None, :] + h_idx[:, None]
    selcol = jnp.zeros((N_HEADS, LANES, LANES), F32).at[h_idx[:, None], src, g_idx[None, :]].set(1.0)
    bif = ml_b_if.reshape(4, N_HEADS).T
    bifcol = jnp.pad(bif, ((0, 0), (0, LANES - 4)))[:, None, :]

    cw = jnp.concatenate([jnp.moveaxis(hd(ml_conv_w[:, :512]), 1, 0), jnp.moveaxis(hd(ml_conv_w[:, 512:]), 1, 0)],
                         axis=2)
    cb = jnp.concatenate([hd(ml_conv_b[:512]), hd(ml_conv_b[512:])], axis=1)[:, None, :]
    mp = dict(w2=w2, b2=b2, selcol=selcol.astype(BF16), bifcol=bifcol, cw=cw, cb=cb,
              gn=gla_norm.reshape(N_HEADS, 1, DV), mn=ml_norm.reshape(N_HEADS, 1, DV))
    return wm, ws, mp


def kernel(x_prompt, x_sample, g_ffn1, w_ffn1_in, w_ffn1_out, g_mix, w_in, gla_w2_fwd, gla_b2_fwd, gla_w2_bwd,
           gla_b2_bwd, gla_norm, ml_conv_w, ml_conv_b, ml_b_if, ml_norm, w_out, g_ffn2, w_ffn2_in, w_ffn2_out,
           g_final):
    depth = w_in.shape[0]
    layers = []
    for l in range(depth):
        layers.append(dict(
            g1=g_ffn1[l][None, :], w1i=w_ffn1_in[l].astype(BF16), w1o=w_ffn1_out[l].astype(BF16),
            gm=g_mix[l][None, :],
            mix=_prep_layer(w_in[l], gla_w2_fwd[l], gla_b2_fwd[l], gla_w2_bwd[l], gla_b2_bwd[l], gla_norm[l],
                            ml_conv_w[l], ml_conv_b[l], ml_b_if[l], ml_norm[l]),
            wo=w_out[l].astype(BF16), g2=g_ffn2[l][None, :], w2i=w_ffn2_in[l].astype(BF16),
            w2o=w_ffn2_out[l].astype(BF16)))
    gf = g_final[None, :]

    def trunk(x):
        bsz, seq, _ = x.shape
        for l, p in enumerate(layers):
            wm, ws, mp = p["mix"]
            x1 = _ffn1(x.reshape(bsz * seq, D_MODEL), p["g1"], p["w1i"], p["w1o"])
            um, sm = _inproj(x1.reshape(bsz, seq, D_MODEL), p["gm"], wm, ws)
            merged = _mixer(um, sm, mp)
            x = _outffn(x1, merged.reshape(bsz * seq, D_MODEL), p["wo"], p["g2"], p["w2i"], p["w2o"], gf,
                        final_norm=(l == depth - 1))
            x = x.reshape(bsz, seq, D_MODEL)
        return x

    return trunk(x_prompt), trunk(x_sample)
```

```python
import functools

import jax
import jax.numpy as jnp
from jax import lax
from jax.experimental import pallas as pl
from jax.experimental.pallas import tpu as pltpu

F32 = jnp.float32
BF16 = jnp.bfloat16

D_MODEL = 1024
D_FF = 2816
N_HEADS = 4
DK = 128
DV = 256
GLA_LOWRANK = 16
GLA_TAU = 16.0
EPS = 1e-6
QK_SCALE = DK ** -0.5

LANES = 128
SUBLANES = 8
CHUNK = 128
GLA_MID = CHUNK // 2
TM_FFN = 512
TF = 256
TN_IN = 512
PAR_UNROLL = 2
GATE_ROWS = 512
GATE_COPIES = 3
VMEM_LIMIT = 56 * 1024 * 1024

GROUPS = ("gqk", "gv", "gr", "mqk", "mv", "mo", "ga", "gb")
N_MAIN = len(GROUPS) * N_HEADS * DV
SM_LR_F, SM_LR_B, SM_IF = 0, GLA_LOWRANK, 2 * GLA_LOWRANK


def _dot(a, b):
    return jnp.dot(a, b, preferred_element_type=F32)


def _dot_nt(a, b):
    return lax.dot_general(a, b, (((1,), (1,)), ((), ())), preferred_element_type=F32)


def _rms(x, g):
    return x * lax.rsqrt(jnp.mean(x * x, axis=-1, keepdims=True) + EPS) * g


def _logsig(z):
    return jnp.minimum(z, 0.0) - jnp.log(1.0 + jnp.exp(-jnp.abs(z)))


def _split_hi_lo(x):
    hi = x.astype(BF16)
    lo = (x - hi.astype(F32)).astype(BF16)
    return hi, lo


def _sel_dot(sel_bf16, x):
    hi, lo = _split_hi_lo(x)
    return _dot(sel_bf16, hi) + _dot(sel_bf16, lo)


def _dot_sel(x, sel_bf16):
    hi, lo = _split_hi_lo(x)
    return _dot(hi, sel_bf16) + _dot(lo, sel_bf16)


def _tile(x):
    return jnp.broadcast_to(x, (SUBLANES, LANES))


def _swiglu_into(h_scr, win_ref, wout_ref, act_scr):
    for j in range(D_FF // TF):
        a = _dot(h_scr[...], win_ref[:, j * TF:(j + 1) * TF])
        g = _dot(h_scr[...], win_ref[:, D_FF + j * TF:D_FF + (j + 1) * TF])
        act_scr[:, j * TF:(j + 1) * TF] = (a * jax.nn.sigmoid(a) * g).astype(BF16)
    return _dot(act_scr[...], wout_ref[...])


def _ffn1_kernel(x_ref, g_ref, win_ref, wout_ref, o_ref, h_scr, act_scr):
    x = x_ref[...]
    h_scr[...] = _rms(x, g_ref[...]).astype(BF16)
    o_ref[...] = x + 0.5 * _swiglu_into(h_scr, win_ref, wout_ref, act_scr)


def _outffn_kernel(x_ref, m_ref, wo_ref, g_ref, win_ref, wout_ref, gf_ref, o_ref, h_scr, act_scr, *, final_norm):
    x = x_ref[...] + _dot(m_ref[...], wo_ref[...])
    h_scr[...] = _rms(x, g_ref[...]).astype(BF16)
    x = x + 0.5 * _swiglu_into(h_scr, win_ref, wout_ref, act_scr)
    o_ref[...] = _rms(x, gf_ref[...]) if final_norm else x


def _inproj_kernel(x_ref, g_ref, wm_ref, ws_ref, um_ref, sm_ref, h_scr):
    h_scr[...] = _rms(x_ref[0], g_ref[...]).astype(BF16)
    for j in range(N_MAIN // TN_IN):
        cols = slice(j * TN_IN, (j + 1) * TN_IN)
        um_ref[0, :, cols] = _dot(h_scr[...], wm_ref[:, cols]).astype(BF16)
    sm_ref[0] = _dot(h_scr[...], ws_ref[...])


def _mixer_kernel(gqk_ref, gv_ref, gr_ref, mqk_ref, mv_ref, mo_ref, ga_ref, gb_ref, sm_ref,
                  w2_ref, b2_ref, selcol_ref, bifcol_ref, cw_ref, cb_ref, gn_ref, mn_ref,
                  out_ref,
                  mqk_scr, bcum_scr, att_scr, qin_scr, kst_t_scr, dcol_scr, gcol_scr, a_scr, g_scr, mprev_scr, mnew_scr,
                  sml_scr, qw_scr, ks_t_scr, floor_scr, dec_scr, og_scr, hm_scr, s_scr, c_scr):
    seq = gqk_ref.shape[1]
    nch = seq // CHUNK
    ri = lax.broadcasted_iota(jnp.int32, (CHUNK, CHUNK), 0)
    ci = lax.broadcasted_iota(jnp.int32, (CHUNK, CHUNK), 1)
    lower = ci <= ri
    upper = ci >= ri
    eye = ci == ri
    lower_bf = lower.astype(BF16)
    upper_bf = upper.astype(BF16)
    ones_blk = jnp.ones((CHUNK, LANES), BF16)
    dirs = ((lower, GLA_MID - 1, CHUNK - 1), (upper, GLA_MID, 0))

    def chunk_rows(c):
        return pl.ds(pl.multiple_of(c * CHUNK, CHUNK), CHUNK)

    def to_col(row):
        return jnp.sum(jnp.where(eye, row, 0.0), axis=1, keepdims=True)

    x = mqk_ref[0].astype(F32)
    rid = lax.broadcasted_iota(jnp.int32, (seq, 2 * DK), 0)
    cid = lax.broadcasted_iota(jnp.int32, (1, 2 * DK), 1)
    prev = jnp.where(rid == 0, 0.0, pltpu.roll(x, 1, 0))
    nxt = jnp.where(rid == seq - 1, 0.0, pltpu.roll(x, seq - 1, 0))
    cw = cw_ref[0]
    conv = cb_ref[0] + prev * cw[0:1] + x * cw[1:2] + nxt * cw[2:3]
    mqk_scr[...] = conv * jax.nn.sigmoid(conv) * jnp.where(cid < DK, QK_SCALE, 1.0)

    def act_body(j, carry):
        rows = pl.ds(pl.multiple_of(j * GATE_ROWS, GATE_ROWS), GATE_ROWS)
        sm = sm_ref[0, rows, :]
        la = _logsig(_dot(sm.astype(BF16), w2_ref[0]) + b2_ref[0]) * (1.0 / GLA_TAU)
        g_c = _dot_sel(sm, selcol_ref[0]) + bifcol_ref[0]
        lf = _logsig(g_c[:, :LANES])
        hm_scr[0, rows, :DK] = la[:, :DK]
        hm_scr[0, rows, DK:] = lf
        hm_scr[1, rows, :DK] = la[:, DK:]
        hm_scr[1, rows, DK:] = lf
        og_scr[0, rows, :] = g_c
        return carry

    lax.fori_loop(0, seq // GATE_ROWS, act_body, 0)

    lane = lax.broadcasted_iota(jnp.int32, (CHUNK, LANES), 1)

    def gate_body(c, carry):
        rows = chunk_rows(c)
        pre = _sel_dot(lower_bf, hm_scr[0, rows, :])
        suf = _sel_dot(upper_bf, hm_scr[1, rows, :])
        gcol = jnp.where(lane % 4 == 1, pre[:, DK:], jnp.where(lane % 4 == 3, suf[:, DK:], og_scr[0, rows, :LANES]))
        gcol_scr[rows, :] = gcol
        bcum_scr[0, rows, :] = pre[:, :DK]
        bcum_scr[1, rows, :] = suf[:, :DK]
        c_max = jnp.max(og_scr[0, rows, LANES:] - gcol, axis=0, keepdims=True)
        for d in (0, 1):
            l_idx = dirs[d][2]
            a = gcol[l_idx:l_idx + 1, :]
            a_scr[d, c] = _tile(a)
            g_scr[d, c] = _tile(a + c_max)
        return carry

    lax.fori_loop(0, nch, gate_body, 0, unroll=PAR_UNROLL)

    def gla_body(c, carry):
        rows = chunk_rows(c)
        qk = gqk_ref[0, rows, :].astype(F32)
        q = qk[:, :DK] * QK_SCALE
        k = qk[:, DK:]
        for d in (0, 1):
            mask, r_idx, l_idx = dirs[d]
            b = bcum_scr[d, rows, :]
            rho = b[r_idx:r_idx + 1, :]
            b_last = b[l_idx:l_idx + 1, :]
            q_mid = (q * jnp.exp(b - rho)).astype(BF16)
            k_mid = (k * jnp.exp(rho - b)).astype(BF16)
            att_scr[d, rows, :] = jnp.where(mask, _dot_nt(q_mid, k_mid), 0.0).astype(BF16)
            qin_scr[d, rows, :] = (q * jnp.exp(b)).astype(BF16)
            kst_t_scr[d, c] = (k * jnp.exp(b_last - b)).astype(BF16).T
            dcol_scr[d, c] = jnp.broadcast_to(to_col(jnp.exp(b_last)), (DK, LANES))
        return carry

    lax.fori_loop(0, nch, gla_body, 0, unroll=PAR_UNROLL)

    def m_body(i, carry):
        m_f, m_b = carry
        cb = nch - 1 - i
        mprev_scr[0, i] = m_f
        mprev_scr[1, cb] = m_b
        m_f = jnp.maximum(a_scr[0, i] + m_f, g_scr[0, i])
        m_b = jnp.maximum(a_scr[1, cb] + m_b, g_scr[1, cb])
        mnew_scr[0, i] = m_f
        mnew_scr[1, cb] = m_b
        return m_f, m_b

    m0 = jnp.full((SUBLANES, LANES), -jnp.inf, F32)
    lax.fori_loop(0, nch, m_body, (m0, m0))

    krow = lax.broadcasted_iota(jnp.int32, (LANES, 2 * LANES), 0)
    ncol = lax.broadcasted_iota(jnp.int32, (LANES, 2 * LANES), 1)

    def selectors(d):
        def picks(k):
            pick_b = ((k < 12) & (k % 4 == 2 * d + 1)).astype(F32)
            return pick_b, ((k < 12) & (k % 4 == 2 * d)).astype(F32) - pick_b
        pick_b, pick_c = picks(krow)
        sel = jnp.where(ncol < LANES, pick_b, pick_c).astype(BF16)
        sel_t = picks(lane)[1].astype(BF16)
        return sel, sel_t

    sels = (selectors(0), selectors(1))

    def prep_body(c, carry):
        rows = chunk_rows(c)
        g = gcol_scr[rows, :]
        g1 = g.astype(BF16).astype(F32)
        r1 = g - g1
        g2 = r1.astype(BF16).astype(F32)
        gs = jnp.where(lane < 4, g1, jnp.where(lane < 8, g2, r1 - g2)).astype(BF16)
        qk = mqk_scr[rows, :]
        q = qk[:, :DK]
        k = qk[:, DK:]
        qk_t = _dot_nt(q.astype(BF16), k.astype(BF16))
        for d in (0, 1):
            mask = dirs[d][0]
            sel, sel_t = sels[d]
            dense = _dot(gs, sel)
            b_t = dense[:, :LANES]
            c_t = dense[:, LANES:]
            c_s = _dot_nt(sel_t, gs)
            stat = slice(2 * d + 1, 2 * d + 2)
            m_prev = mprev_scr[d, c][0:1, stat]
            m_new = mnew_scr[d, c][0:1, stat]
            b_last = a_scr[d, c][0:1, stat]
            log_d = jnp.where(mask, b_t + c_s, -jnp.inf)
            m_inter = b_t + m_prev
            m_t = jnp.maximum(m_inter, jnp.max(log_d, axis=1, keepdims=True))
            sml_scr[d, rows, :] = (qk_t * jnp.exp(log_d - m_t)).astype(BF16)
            qw_scr[d, rows, :] = (jnp.exp(m_inter - m_t) * q).astype(BF16)
            floor_scr[d, rows, :] = jnp.exp(-m_t)
            w_st = jnp.exp(b_last + c_t - m_new)
            ks_t_scr[d, c] = (k * w_st).astype(BF16).T
            dec_scr[d, c] = _tile(jnp.exp(b_last + m_prev - m_new))
        return carry

    lax.fori_loop(0, nch, prep_body, 0, unroll=PAR_UNROLL)

    s_scr[...] = jnp.zeros_like(s_scr)
    c_scr[...] = jnp.zeros_like(c_scr)

    def scan_body(i, carry):
        for d, c in ((0, i), (1, nch - 1 - i)):
            rows = chunk_rows(c)
            v = gv_ref[0, rows, :]
            s = s_scr[d]
            og_scr[d, rows, :] = _dot(jnp.concatenate([att_scr[d, rows, :], qin_scr[d, rows, :]], axis=1),
                                      jnp.concatenate([v, s.astype(BF16)], axis=0))
            dcol = dcol_scr[d, c]
            s_scr[d] = s * jnp.concatenate([dcol, dcol], axis=1) + _dot(kst_t_scr[d, c], v)
            v_aug = jnp.concatenate([mv_ref[0, rows, :], ones_blk], axis=1)
            cst = c_scr[d]
            out = _dot(jnp.concatenate([sml_scr[d, rows, :], qw_scr[d, rows, :]], axis=1),
                       jnp.concatenate([v_aug, cst.astype(BF16)], axis=0))
            rden = 1.0 / jnp.maximum(jnp.abs(out[:, DV:]), floor_scr[d, rows, :])
            hm_scr[d, rows, :] = out[:, :DV] * jnp.concatenate([rden, rden], axis=1)
            c_scr[d] = dec_scr[d, c][0:1, 0:1] * cst + _dot(ks_t_scr[d, c], v_aug)
        return carry

    lax.fori_loop(0, nch, scan_body, 0)

    def merge_body(c, carry):
        rows = chunk_rows(c)
        og = og_scr[0, rows, :] + og_scr[1, rows, :]
        hm = hm_scr[0, rows, :] + hm_scr[1, rows, :]
        r = gr_ref[0, rows, :].astype(F32)
        o_a = (og * lax.rsqrt(jnp.mean(og * og, axis=-1, keepdims=True) + EPS) * gn_ref[0]) * (r * jax.nn.sigmoid(r))
        o_b = (hm * lax.rsqrt(jnp.mean(hm * hm, axis=-1, keepdims=True) + EPS) * mn_ref[0]) \
            * jax.nn.sigmoid(mo_ref[0, rows, :].astype(F32))
        merged = jax.nn.sigmoid(ga_ref[0, rows, :].astype(F32)) * o_a \
            + jax.nn.sigmoid(gb_ref[0, rows, :].astype(F32)) * o_b
        out_ref[0, rows, :] = merged.astype(BF16)
        return carry

    lax.fori_loop(0, nch, merge_body, 0, unroll=PAR_UNROLL)


def _resident(shape):
    nd = len(shape)
    return pl.BlockSpec(shape, lambda *_: (0,) * nd, pipeline_mode=pl.Buffered(1))


def _params(n_axes):
    return pltpu.CompilerParams(dimension_semantics=("arbitrary",) * n_axes, vmem_limit_bytes=VMEM_LIMIT)


def _ffn1(x2d, g, win, wout):
    m = x2d.shape[0]
    tile = pl.BlockSpec((TM_FFN, D_MODEL), lambda i: (i, 0))
    return pl.pallas_call(
        _ffn1_kernel,
        grid=(m // TM_FFN,),
        in_specs=[tile, _resident(g.shape), _resident(win.shape), _resident(wout.shape)],
        out_specs=tile,
        out_shape=jax.ShapeDtypeStruct((m, D_MODEL), F32),
        scratch_shapes=[pltpu.VMEM((TM_FFN, D_MODEL), BF16), pltpu.VMEM((TM_FFN, D_FF), BF16)],
        compiler_params=_params(1),
        name="ffn1",
    )(x2d, g, win, wout)


def _outffn(x2d, merged2d, wo, g, win, wout, gf, final_norm):
    m = x2d.shape[0]
    tile = pl.BlockSpec((TM_FFN, D_MODEL), lambda i: (i, 0))
    return pl.pallas_call(
        functools.partial(_outffn_kernel, final_norm=final_norm),
        grid=(m // TM_FFN,),
        in_specs=[tile, tile, _resident(wo.shape), _resident(g.shape), _resident(win.shape), _resident(wout.shape),
                  _resident(gf.shape)],
        out_specs=tile,
        out_shape=jax.ShapeDtypeStruct((m, D_MODEL), F32),
        scratch_shapes=[pltpu.VMEM((TM_FFN, D_MODEL), BF16), pltpu.VMEM((TM_FFN, D_FF), BF16)],
        compiler_params=_params(1),
        name="outffn",
    )(x2d, merged2d, wo, g, win, wout, gf)


def _inproj(x, g, wm, ws):
    bsz, seq, _ = x.shape
    tm = TM_FFN
    return pl.pallas_call(
        _inproj_kernel,
        grid=(bsz, seq // tm),
        in_specs=[pl.BlockSpec((1, tm, D_MODEL), lambda b, i: (b, i, 0)),
                  _resident(g.shape), _resident(wm.shape), _resident(ws.shape)],
        out_specs=[pl.BlockSpec((1, tm, N_MAIN), lambda b, i: (b, i, 0)),
                   pl.BlockSpec((1, tm, LANES), lambda b, i: (b, i, 0))],
        out_shape=[jax.ShapeDtypeStruct((bsz, seq, N_MAIN), BF16),
                   jax.ShapeDtypeStruct((bsz, seq, LANES), F32)],
        scratch_shapes=[pltpu.VMEM((tm, D_MODEL), BF16)],
        compiler_params=_params(2),
        name="inproj",
    )(x, g, wm, ws)


def _mixer(um, sm, mp):
    bsz, seq, _ = um.shape
    nch = seq // CHUNK

    def group(name):
        base = GROUPS.index(name) * N_HEADS
        return pl.BlockSpec((1, seq, DV), lambda b, h: (b, 0, base + h))

    def head(arr):
        nd = arr.ndim
        return pl.BlockSpec((1,) + arr.shape[1:], lambda b, h: (h,) + (0,) * (nd - 1))

    plist = [mp[n] for n in ("w2", "b2", "selcol", "bifcol", "cw", "cb", "gn", "mn")]
    in_specs = [group(n) for n in GROUPS] + [pl.BlockSpec((1, seq, LANES), lambda b, h: (b, 0, 0))] \
        + [head(p) for p in plist]
    tok_bf = pltpu.VMEM((2, seq, DK), BF16)
    chunk_t_bf = pltpu.VMEM((2, nch, DK, CHUNK), BF16)
    chunk_tile = pltpu.VMEM((2, nch, SUBLANES, LANES), F32)
    return pl.pallas_call(
        _mixer_kernel,
        grid=(bsz, N_HEADS),
        in_specs=in_specs,
        out_specs=pl.BlockSpec((1, seq, DV), lambda b, h: (b, 0, h)),
        out_shape=jax.ShapeDtypeStruct((bsz, seq, N_HEADS * DV), BF16),
        scratch_shapes=[pltpu.VMEM((seq, 2 * DK), F32),
                        pltpu.VMEM((2, seq, DK), F32),
                        tok_bf, tok_bf, chunk_t_bf,
                        pltpu.VMEM((2, nch, DK, LANES), F32),
                        pltpu.VMEM((seq, LANES), F32),
                        chunk_tile, chunk_tile, chunk_tile, chunk_tile,
                        tok_bf, tok_bf, chunk_t_bf,
                        pltpu.VMEM((2, seq, LANES), F32),
                        chunk_tile,
                        pltpu.VMEM((2, seq, DV), F32), pltpu.VMEM((2, seq, DV), F32),
                        pltpu.VMEM((2, DK, DV), F32), pltpu.VMEM((2, DK, DV + LANES), F32)],
        compiler_params=_params(2),
        name="mixer",
    )(*([um] * len(GROUPS)), sm, *plist)


def _prep_layer(w_in, gla_w2_fwd, gla_b2_fwd, gla_w2_bwd, gla_b2_bwd, gla_norm, ml_conv_w, ml_conv_b, ml_b_if,
                ml_norm):
    sizes = (512, 512, 1024, 1024, 32, 1024, 1024, 1024, 16, 1024, 1024)
    offs = [0]
    for s in sizes:
        offs.append(offs[-1] + s)
    gq, gk, gv, gr, glr, mqk, mv, mo, mif, gate_a, gate_b = [w_in[:, offs[i]:offs[i + 1]] for i in range(len(sizes))]

    def pair(a, b):
        hd = lambda w: w.reshape(D_MODEL, N_HEADS, DK)
        return jnp.concatenate([hd(a), hd(b)], axis=2).reshape(D_MODEL, N_HEADS * 2 * DK)

    parts = dict(gqk=pair(gq, gk), gv=gv, gr=gr, mqk=pair(mqk[:, :512], mqk[:, 512:]), mv=mv, mo=mo, ga=gate_a,
                 gb=gate_b)
    wm = jnp.concatenate([parts[n] for n in GROUPS], axis=1).astype(BF16)
    ws = jnp.concatenate([glr, mif, jnp.zeros((D_MODEL, LANES - 48), F32)], axis=1).astype(BF16)

    hd = lambda a: a.reshape(a.shape[:-1] + (N_HEADS, DK))

    def w2pad(w2, row0):
        w = jnp.moveaxis(hd(w2), 1, 0)
        return jnp.pad(w, ((0, 0), (row0, LANES - row0 - GLA_LOWRANK), (0, 0)))

    w2 = jnp.concatenate([w2pad(gla_w2_fwd, SM_LR_F), w2pad(gla_w2_bwd, SM_LR_B)], axis=2).astype(BF16)
    b2 = jnp.concatenate([hd(gla_b2_fwd), hd(gla_b2_bwd)], axis=1)[:, None, :]

    g_idx = jnp.arange(4)
    h_idx = jnp.arange(N_HEADS)
    src = SM_IF + 4 * g_idx[None, :] + h_idx[:, None]
    dst = jnp.concatenate([4 * r + g_idx for r in range(GATE_COPIES)] + [jnp.array([LANES + 1, LANES + 3])])
    gsrc = jnp.concatenate([g_idx] * GATE_COPIES + [jnp.array([0, 2])])
    selcol = jnp.zeros((N_HEADS, LANES, 2 * LANES), F32).at[h_idx[:, None], src[:, gsrc], dst[None, :]].set(1.0)
    bif = ml_b_if.reshape(4, N_HEADS).T
    bifcol = jnp.zeros((N_HEADS, 2 * LANES), F32).at[:, dst].set(bif[:, gsrc])[:, None, :]

    cw = jnp.concatenate([jnp.moveaxis(hd(ml_conv_w[:, :512]), 1, 0), jnp.moveaxis(hd(ml_conv_w[:, 512:]), 1, 0)],
                         axis=2)
    cb = jnp.concatenate([hd(ml_conv_b[:512]), hd(ml_conv_b[512:])], axis=1)[:, None, :]
    mp = dict(w2=w2, b2=b2, selcol=selcol.astype(BF16), bifcol=bifcol, cw=cw, cb=cb,
              gn=gla_norm.reshape(N_HEADS, 1, DV), mn=ml_norm.reshape(N_HEADS, 1, DV))
    return wm, ws, mp


def kernel(x_prompt, x_sample, g_ffn1, w_ffn1_in, w_ffn1_out, g_mix, w_in, gla_w2_fwd, gla_b2_fwd, gla_w2_bwd,
           gla_b2_bwd, gla_norm, ml_conv_w, ml_conv_b, ml_b_if, ml_norm, w_out, g_ffn2, w_ffn2_in, w_ffn2_out,
           g_final):
    depth = w_in.shape[0]
    layers = []
    for l in range(depth):
        layers.append(dict(
            g1=g_ffn1[l][None, :], w1i=w_ffn1_in[l].astype(BF16), w1o=w_ffn1_out[l].astype(BF16),
            gm=g_mix[l][None, :],
            mix=_prep_layer(w_in[l], gla_w2_fwd[l], gla_b2_fwd[l], gla_w2_bwd[l], gla_b2_bwd[l], gla_norm[l],
                            ml_conv_w[l], ml_conv_b[l], ml_b_if[l], ml_norm[l]),
            wo=w_out[l].astype(BF16), g2=g_ffn2[l][None, :], w2i=w_ffn2_in[l].astype(BF16),
            w2o=w_ffn2_out[l].astype(BF16)))
    gf = g_final[None, :]

    def trunk(x):
        bsz, seq, _ = x.shape
        for l, p in enumerate(layers):
            wm, ws, mp = p["mix"]
            x1 = _ffn1(x.reshape(bsz * seq, D_MODEL), p["g1"], p["w1i"], p["w1o"])
            um, sm = _inproj(x1.reshape(bsz, seq, D_MODEL), p["gm"], wm, ws)
            merged = _mixer(um, sm, mp)
            x = _outffn(x1, merged.reshape(bsz * seq, D_MODEL), p["wo"], p["g2"], p["w2i"], p["w2o"], gf,
                        final_norm=(l == depth - 1))
            x = x.reshape(bsz, seq, D_MODEL)
        return x

    return trunk(x_prompt), trunk(x_sample)
```

```python
import functools

import jax
import jax.numpy as jnp
from jax import lax
from jax.experimental import pallas as pl
from jax.experimental.pallas import tpu as pltpu

F32 = jnp.float32
BF16 = jnp.bfloat16

D_MODEL = 1024
D_FF = 2816
N_HEADS = 4
DK = 128
DV = 256
GLA_LOWRANK = 16
GLA_TAU = 16.0
EPS = 1e-6
QK_SCALE = DK ** -0.5
LOG2E = 1.4426950408889634

LANES = 128
SUBLANES = 8
CHUNK = 128
GLA_MID = CHUNK // 2
TM_FFN = 512
TF = 256
TN_IN = 512
PAR_UNROLL = 2
GATE_ROWS = 512
GATE_COPIES = 3
VMEM_LIMIT = 56 * 1024 * 1024

GROUPS = ("gqk", "gv", "gr", "mqk", "mv", "mo", "ga", "gb")
GROUP_ACT = dict(gqk=None, gv=None, gr="silu", mqk=None, mv=None, mo="sigmoid", ga="sigmoid", gb="sigmoid")
N_MAIN = len(GROUPS) * N_HEADS * DV
SM_LR_F, SM_LR_B, SM_IF = 0, GLA_LOWRANK, 2 * GLA_LOWRANK


def _dot(a, b):
    return jnp.dot(a, b, preferred_element_type=F32)


def _dot_nt(a, b):
    return lax.dot_general(a, b, (((1,), (1,)), ((), ())), preferred_element_type=F32)


def _rms(x, g):
    return x * lax.rsqrt(jnp.mean(x * x, axis=-1, keepdims=True) + EPS) * g


def _logsig(z):
    return jnp.minimum(z, 0.0) - jnp.log(1.0 + jnp.exp(-jnp.abs(z)))


def _split_hi_lo(x):
    hi = x.astype(BF16)
    lo = (x - hi.astype(F32)).astype(BF16)
    return hi, lo


def _sel_dot(sel_bf16, x):
    hi, lo = _split_hi_lo(x)
    return _dot(sel_bf16, hi) + _dot(sel_bf16, lo)


def _dot_sel(x, sel_bf16):
    hi, lo = _split_hi_lo(x)
    return _dot(hi, sel_bf16) + _dot(lo, sel_bf16)


def _tile(x):
    return jnp.broadcast_to(x, (SUBLANES, LANES))


def _swiglu_into(h_scr, win_ref, wout_ref, act_scr):
    for j in range(D_FF // TF):
        a = _dot(h_scr[...], win_ref[:, j * TF:(j + 1) * TF])
        g = _dot(h_scr[...], win_ref[:, D_FF + j * TF:D_FF + (j + 1) * TF])
        act_scr[:, j * TF:(j + 1) * TF] = (a * jax.nn.sigmoid(a) * g).astype(BF16)
    return _dot(act_scr[...], wout_ref[...])


def _ffn1_kernel(x_ref, g_ref, win_ref, wout_ref, o_ref, h_scr, act_scr):
    x = x_ref[...]
    h_scr[...] = _rms(x, g_ref[...]).astype(BF16)
    o_ref[...] = x + 0.5 * _swiglu_into(h_scr, win_ref, wout_ref, act_scr)


def _outffn_kernel(x_ref, m_ref, wo_ref, g_ref, win_ref, wout_ref, gf_ref, o_ref, h_scr, act_scr, *, final_norm):
    x = x_ref[...] + _dot(m_ref[...], wo_ref[...])
    h_scr[...] = _rms(x, g_ref[...]).astype(BF16)
    x = x + 0.5 * _swiglu_into(h_scr, win_ref, wout_ref, act_scr)
    o_ref[...] = _rms(x, gf_ref[...]) if final_norm else x


def _inproj_kernel(x_ref, g_ref, wm_ref, ws_ref, um_ref, sm_ref, h_scr):
    h_scr[...] = _rms(x_ref[0], g_ref[...]).astype(BF16)
    for j in range(N_MAIN // TN_IN):
        cols = slice(j * TN_IN, (j + 1) * TN_IN)
        u = _dot(h_scr[...], wm_ref[:, cols])
        act = GROUP_ACT[GROUPS[j * TN_IN // (N_HEADS * DV)]]
        if act == "silu":
            u = u * jax.nn.sigmoid(u)
        elif act == "sigmoid":
            u = jax.nn.sigmoid(u)
        um_ref[0, :, cols] = u.astype(BF16)
    sm_ref[0] = _dot(h_scr[...], ws_ref[...])


def _mixer_kernel(gqk_ref, gv_ref, gr_ref, mqk_ref, mv_ref, mo_ref, ga_ref, gb_ref, sm_ref,
                  w2_ref, b2_ref, selcol_ref, bifcol_ref, cw_ref, cb_ref, gn_ref, mn_ref,
                  out_ref,
                  mqk_scr, bcum_scr, att_scr, qin_scr, kst_t_scr, dcol_scr, gcol_scr, a_scr, g_scr, mprev_scr, mnew_scr,
                  sml_scr, qw_scr, ks_t_scr, floor_scr, dint_scr, dec_scr, dn_scr, og_scr, hm_scr, s_scr, c_scr, n_scr):
    seq = gqk_ref.shape[1]
    nch = seq // CHUNK
    ri = lax.broadcasted_iota(jnp.int32, (CHUNK, CHUNK), 0)
    ci = lax.broadcasted_iota(jnp.int32, (CHUNK, CHUNK), 1)
    lower = ci <= ri
    upper = ci >= ri
    eye = ci == ri
    lower_bf = lower.astype(BF16)
    upper_bf = upper.astype(BF16)
    dirs = ((lower, GLA_MID - 1, CHUNK - 1), (upper, GLA_MID, 0))

    def chunk_rows(c):
        return pl.ds(pl.multiple_of(c * CHUNK, CHUNK), CHUNK)

    def to_col(row):
        return jnp.sum(jnp.where(eye, row, 0.0), axis=1, keepdims=True)

    x = mqk_ref[0].astype(F32)
    cid = lax.broadcasted_iota(jnp.int32, (1, 2 * DK), 1)
    q_scale = jnp.where(cid < DK, QK_SCALE, 1.0)
    prev = pltpu.roll(x, 1, 0)
    nxt = pltpu.roll(x, seq - 1, 0)
    cw = cw_ref[0]

    def conv_silu(p, xc, n):
        cv = cb_ref[0] + p * cw[0:1] + xc * cw[1:2] + n * cw[2:3]
        return cv * jax.nn.sigmoid(cv) * q_scale

    mqk_scr[...] = conv_silu(prev, x, nxt)
    r8 = lax.broadcasted_iota(jnp.int32, (SUBLANES, 2 * DK), 0)
    top = slice(0, SUBLANES)
    bot = slice(seq - SUBLANES, seq)
    mqk_scr[top, :] = conv_silu(jnp.where(r8 == 0, 0.0, prev[top]), x[top], nxt[top])
    mqk_scr[bot, :] = conv_silu(prev[bot], x[bot], jnp.where(r8 == SUBLANES - 1, 0.0, nxt[bot]))

    def act_body(j, carry):
        rows = pl.ds(pl.multiple_of(j * GATE_ROWS, GATE_ROWS), GATE_ROWS)
        sm = sm_ref[0, rows, :]
        la = _logsig(_dot(sm.astype(BF16), w2_ref[0]) + b2_ref[0]) * (LOG2E / GLA_TAU)
        g_c = _dot_sel(sm, selcol_ref[0]) + bifcol_ref[0]
        lf = _logsig(g_c[:, :LANES]) * LOG2E
        hm_scr[0, rows, :DK] = la[:, :DK]
        hm_scr[0, rows, DK:] = lf
        hm_scr[1, rows, :DK] = la[:, DK:]
        hm_scr[1, rows, DK:] = lf
        og_scr[0, rows, :] = g_c * LOG2E
        return carry

    lax.fori_loop(0, seq // GATE_ROWS, act_body, 0)

    lane = lax.broadcasted_iota(jnp.int32, (CHUNK, LANES), 1)

    def gate_chunk(c, slot):
        rows = chunk_rows(c)
        pre = _sel_dot(lower_bf, hm_scr[0, rows, :])
        suf = _sel_dot(upper_bf, hm_scr[1, rows, :])
        gcol = jnp.where(lane % 4 == 1, pre[:, DK:], jnp.where(lane % 4 == 3, suf[:, DK:], og_scr[0, rows, :LANES]))
        gcol_scr[rows, :] = gcol
        bcum_scr[slot, 0] = pre[:, :DK]
        bcum_scr[slot, 1] = suf[:, :DK]
        c_max = jnp.max(og_scr[0, rows, LANES:] - gcol, axis=0, keepdims=True)
        for d in (0, 1):
            l_idx = dirs[d][2]
            a = gcol[l_idx:l_idx + 1, :]
            a_scr[d, c] = _tile(a)
            g_scr[d, c] = _tile(a + c_max)

    def gla_chunk(c, slot):
        rows = chunk_rows(c)
        qk = gqk_ref[0, rows, :].astype(F32)
        q = qk[:, :DK] * QK_SCALE
        k = qk[:, DK:]
        for d in (0, 1):
            mask, r_idx, l_idx = dirs[d]
            b = bcum_scr[slot, d]
            rho = b[r_idx:r_idx + 1, :]
            b_last = b[l_idx:l_idx + 1, :]
            q_mid = (q * jnp.exp2(b - rho)).astype(BF16)
            k_mid = (k * jnp.exp2(rho - b)).astype(BF16)
            att_scr[d, rows, :] = jnp.where(mask, _dot_nt(q_mid, k_mid), 0.0).astype(BF16)
            qin_scr[d, rows, :] = (q * jnp.exp2(b)).astype(BF16)
            kst_t_scr[d, c] = (k * jnp.exp2(b_last - b)).astype(BF16).T
            dcol_scr[d, c] = jnp.broadcast_to(to_col(jnp.exp2(b_last)), (DK, LANES))

    gate_chunk(0, 0)

    def gate_gla_body(p, carry):
        for slot in (0, 1):
            c = 2 * p + slot
            gate_chunk(jnp.minimum(c + 1, nch - 1), 1 - slot)
            gla_chunk(c, slot)
        return carry

    lax.fori_loop(0, nch // 2, gate_gla_body, 0)

    def m_body(i, carry):
        m_f, m_b = carry
        cb = nch - 1 - i
        mprev_scr[0, i] = m_f
        mprev_scr[1, cb] = m_b
        m_f = jnp.maximum(a_scr[0, i] + m_f, g_scr[0, i])
        m_b = jnp.maximum(a_scr[1, cb] + m_b, g_scr[1, cb])
        mnew_scr[0, i] = m_f
        mnew_scr[1, cb] = m_b
        return m_f, m_b

    m0 = jnp.full((SUBLANES, LANES), -jnp.inf, F32)
    lax.fori_loop(0, nch, m_body, (m0, m0))

    krow = lax.broadcasted_iota(jnp.int32, (LANES, 2 * LANES), 0)
    ncol = lax.broadcasted_iota(jnp.int32, (LANES, 2 * LANES), 1)

    def selectors(d):
        def picks(k):
            pick_b = ((k < 12) & (k % 4 == 2 * d + 1)).astype(F32)
            return pick_b, ((k < 12) & (k % 4 == 2 * d)).astype(F32) - pick_b
        pick_b, pick_c = picks(krow)
        sel = jnp.where(ncol < LANES, pick_b, pick_c).astype(BF16)
        sel_t = picks(lane)[1].astype(BF16)
        return sel, sel_t

    sels = (selectors(0), selectors(1))

    def prep_dir(d, c, slot):
        rows = chunk_rows(c)
        g = gcol_scr[rows, :]
        g1 = g.astype(BF16).astype(F32)
        r1 = g - g1
        g2 = r1.astype(BF16).astype(F32)
        gs = jnp.where(lane < 4, g1, jnp.where(lane < 8, g2, r1 - g2)).astype(BF16)
        qk = mqk_scr[rows, :]
        q = qk[:, :DK]
        k = qk[:, DK:]
        qk_t = _dot_nt(q.astype(BF16), k.astype(BF16))
        mask = dirs[d][0]
        sel, sel_t = sels[d]
        dense = _dot(gs, sel)
        b_t = dense[:, :LANES]
        c_t = dense[:, LANES:]
        c_s = _dot_nt(sel_t, gs)
        stat = slice(2 * d + 1, 2 * d + 2)
        m_prev = mprev_scr[d, c][0:1, stat]
        m_new = mnew_scr[d, c][0:1, stat]
        b_last = a_scr[d, c][0:1, stat]
        log_d = jnp.where(mask, b_t + c_s, -jnp.inf)
        m_inter = b_t + m_prev
        m_t = jnp.maximum(m_inter, jnp.max(log_d, axis=1, keepdims=True))
        sml = qk_t * jnp.exp2(log_d - m_t)
        sml_scr[slot, d] = sml.astype(BF16)
        dint_scr[slot, d] = jnp.broadcast_to(jnp.sum(sml, axis=1, keepdims=True), (CHUNK, LANES))
        qw_scr[slot, d] = (jnp.exp2(m_inter - m_t) * q).astype(BF16)
        floor_scr[slot, d] = jnp.exp2(-m_t)
        ks = k * jnp.exp2(b_last + c_t - m_new)
        ks_t_scr[slot, d] = ks.astype(BF16).T
        dn_scr[slot, d] = _tile(jnp.sum(ks, axis=0, keepdims=True))
        dec_scr[slot, d] = _tile(jnp.exp2(b_last + m_prev - m_new))

    def scan_dir(d, c, slot):
        rows = chunk_rows(c)
        v = gv_ref[0, rows, :]
        s = s_scr[d]
        og_scr[d, rows, :] = _dot(jnp.concatenate([att_scr[d, rows, :], qin_scr[d, rows, :]], axis=1),
                                  jnp.concatenate([v, s.astype(BF16)], axis=0))
        dcol = dcol_scr[d, c]
        s_scr[d] = s * jnp.concatenate([dcol, dcol], axis=1) + _dot(kst_t_scr[d, c], v)
        mv = mv_ref[0, rows, :]
        cst = c_scr[d]
        nst = n_scr[d]
        qw = qw_scr[slot, d]
        num = _dot(jnp.concatenate([sml_scr[slot, d], qw], axis=1), jnp.concatenate([mv, cst.astype(BF16)], axis=0))
        den = dint_scr[slot, d] + jnp.sum(qw.astype(F32) * nst[0:1, :], axis=1, keepdims=True)
        rden = 1.0 / jnp.maximum(jnp.abs(den), floor_scr[slot, d])
        hm_scr[d, rows, :] = num * jnp.concatenate([rden, rden], axis=1)
        decay = dec_scr[slot, d][0:1, 0:1]
        c_scr[d] = decay * cst + _dot(ks_t_scr[slot, d], mv)
        n_scr[d] = decay * nst + dn_scr[slot, d]

    s_scr[...] = jnp.zeros_like(s_scr)
    c_scr[...] = jnp.zeros_like(c_scr)
    n_scr[...] = jnp.zeros_like(n_scr)
    prep_dir(0, 0, 0)
    prep_dir(1, nch - 1, 0)

    def scan_body(p, carry):
        for slot in (0, 1):
            i = 2 * p + slot
            prep_dir(0, jnp.minimum(i + 1, nch - 1), 1 - slot)
            prep_dir(1, jnp.maximum(nch - 2 - i, 0), 1 - slot)
            scan_dir(0, i, slot)
            scan_dir(1, nch - 1 - i, slot)
        return carry

    lax.fori_loop(0, nch // 2, scan_body, 0)

    def merge_body(c, carry):
        rows = chunk_rows(c)
        og = og_scr[0, rows, :] + og_scr[1, rows, :]
        hm = hm_scr[0, rows, :] + hm_scr[1, rows, :]
        o_a = (og * lax.rsqrt(jnp.mean(og * og, axis=-1, keepdims=True) + EPS) * gn_ref[0]) \
            * gr_ref[0, rows, :].astype(F32)
        o_b = (hm * lax.rsqrt(jnp.mean(hm * hm, axis=-1, keepdims=True) + EPS) * mn_ref[0]) \
            * mo_ref[0, rows, :].astype(F32)
        merged = ga_ref[0, rows, :].astype(F32) * o_a + gb_ref[0, rows, :].astype(F32) * o_b
        out_ref[0, rows, :] = merged.astype(BF16)
        return carry

    lax.fori_loop(0, nch, merge_body, 0, unroll=PAR_UNROLL)


def _resident(shape):
    nd = len(shape)
    return pl.BlockSpec(shape, lambda *_: (0,) * nd, pipeline_mode=pl.Buffered(1))


def _params(n_axes):
    return pltpu.CompilerParams(dimension_semantics=("arbitrary",) * n_axes, vmem_limit_bytes=VMEM_LIMIT)


def _ffn1(x2d, g, win, wout):
    m = x2d.shape[0]
    tile = pl.BlockSpec((TM_FFN, D_MODEL), lambda i: (i, 0))
    return pl.pallas_call(
        _ffn1_kernel,
        grid=(m // TM_FFN,),
        in_specs=[tile, _resident(g.shape), _resident(win.shape), _resident(wout.shape)],
        out_specs=tile,
        out_shape=jax.ShapeDtypeStruct((m, D_MODEL), F32),
        scratch_shapes=[pltpu.VMEM((TM_FFN, D_MODEL), BF16), pltpu.VMEM((TM_FFN, D_FF), BF16)],
        compiler_params=_params(1),
        name="ffn1",
    )(x2d, g, win, wout)


def _outffn(x2d, merged2d, wo, g, win, wout, gf, final_norm):
    m = x2d.shape[0]
    tile = pl.BlockSpec((TM_FFN, D_MODEL), lambda i: (i, 0))
    return pl.pallas_call(
        functools.partial(_outffn_kernel, final_norm=final_norm),
        grid=(m // TM_FFN,),
        in_specs=[tile, tile, _resident(wo.shape), _resident(g.shape), _resident(win.shape), _resident(wout.shape),
                  _resident(gf.shape)],
        out_specs=tile,
        out_shape=jax.ShapeDtypeStruct((m, D_MODEL), F32),
        scratch_shapes=[pltpu.VMEM((TM_FFN, D_MODEL), BF16), pltpu.VMEM((TM_FFN, D_FF), BF16)],
        compiler_params=_params(1),
        name="outffn",
    )(x2d, merged2d, wo, g, win, wout, gf)


def _inproj(x, g, wm, ws):
    bsz, seq, _ = x.shape
    tm = TM_FFN
    return pl.pallas_call(
        _inproj_kernel,
        grid=(bsz, seq // tm),
        in_specs=[pl.BlockSpec((1, tm, D_MODEL), lambda b, i: (b, i, 0)),
                  _resident(g.shape), _resident(wm.shape), _resident(ws.shape)],
        out_specs=[pl.BlockSpec((1, tm, N_MAIN), lambda b, i: (b, i, 0)),
                   pl.BlockSpec((1, tm, LANES), lambda b, i: (b, i, 0))],
        out_shape=[jax.ShapeDtypeStruct((bsz, seq, N_MAIN), BF16),
                   jax.ShapeDtypeStruct((bsz, seq, LANES), F32)],
        scratch_shapes=[pltpu.VMEM((tm, D_MODEL), BF16)],
        compiler_params=_params(2),
        name="inproj",
    )(x, g, wm, ws)


def _mixer(um, sm, mp):
    bsz, seq, _ = um.shape
    nch = seq // CHUNK

    def group(name):
        base = GROUPS.index(name) * N_HEADS
        return pl.BlockSpec((1, seq, DV), lambda b, h: (b, 0, base + h))

    def head(arr):
        nd = arr.ndim
        return pl.BlockSpec((1,) + arr.shape[1:], lambda b, h: (h,) + (0,) * (nd - 1))

    plist = [mp[n] for n in ("w2", "b2", "selcol", "bifcol", "cw", "cb", "gn", "mn")]
    in_specs = [group(n) for n in GROUPS] + [pl.BlockSpec((1, seq, LANES), lambda b, h: (b, 0, 0))] \
        + [head(p) for p in plist]
    tok_bf = pltpu.VMEM((2, seq, DK), BF16)
    chunk_t_bf = pltpu.VMEM((2, nch, DK, CHUNK), BF16)
    chunk_tile = pltpu.VMEM((2, nch, SUBLANES, LANES), F32)
    slot_bf = pltpu.VMEM((2, 2, CHUNK, CHUNK), BF16)
    slot_f32 = pltpu.VMEM((2, 2, CHUNK, LANES), F32)
    slot_tile = pltpu.VMEM((2, 2, SUBLANES, LANES), F32)
    return pl.pallas_call(
        _mixer_kernel,
        grid=(bsz, N_HEADS),
        in_specs=in_specs,
        out_specs=pl.BlockSpec((1, seq, DV), lambda b, h: (b, 0, h)),
        out_shape=jax.ShapeDtypeStruct((bsz, seq, N_HEADS * DV), BF16),
        scratch_shapes=[pltpu.VMEM((seq, 2 * DK), F32),
                        slot_f32,
                        tok_bf, tok_bf, chunk_t_bf,
                        pltpu.VMEM((2, nch, DK, LANES), F32),
                        pltpu.VMEM((seq, LANES), F32),
                        chunk_tile, chunk_tile, chunk_tile, chunk_tile,
                        slot_bf, slot_bf, slot_bf,
                        slot_f32, slot_f32,
                        slot_tile, slot_tile,
                        pltpu.VMEM((2, seq, DV), F32), pltpu.VMEM((2, seq, DV), F32),
                        pltpu.VMEM((2, DK, DV), F32), pltpu.VMEM((2, DK, DV), F32),
                        pltpu.VMEM((2, SUBLANES, DK), F32)],
        compiler_params=_params(2),
        name="mixer",
    )(*([um] * len(GROUPS)), sm, *plist)


def _prep_layer(w_in, gla_w2_fwd, gla_b2_fwd, gla_w2_bwd, gla_b2_bwd, gla_norm, ml_conv_w, ml_conv_b, ml_b_if,
                ml_norm):
    sizes = (512, 512, 1024, 1024, 32, 1024, 1024, 1024, 16, 1024, 1024)
    offs = [0]
    for s in sizes:
        offs.append(offs[-1] + s)
    gq, gk, gv, gr, glr, mqk, mv, mo, mif, gate_a, gate_b = [w_in[:, offs[i]:offs[i + 1]] for i in range(len(sizes))]

    def pair(a, b):
        hd = lambda w: w.reshape(D_MODEL, N_HEADS, DK)
        return jnp.concatenate([hd(a), hd(b)], axis=2).reshape(D_MODEL, N_HEADS * 2 * DK)

    parts = dict(gqk=pair(gq, gk), gv=gv, gr=gr, mqk=pair(mqk[:, :512], mqk[:, 512:]), mv=mv, mo=mo, ga=gate_a,
                 gb=gate_b)
    wm = jnp.concatenate([parts[n] for n in GROUPS], axis=1).astype(BF16)
    ws = jnp.concatenate([glr, mif, jnp.zeros((D_MODEL, LANES - 48), F32)], axis=1).astype(BF16)

    hd = lambda a: a.reshape(a.shape[:-1] + (N_HEADS, DK))

    def w2pad(w2, row0):
        w = jnp.moveaxis(hd(w2), 1, 0)
        return jnp.pad(w, ((0, 0), (row0, LANES - row0 - GLA_LOWRANK), (0, 0)))

    w2 = jnp.concatenate([w2pad(gla_w2_fwd, SM_LR_F), w2pad(gla_w2_bwd, SM_LR_B)], axis=2).astype(BF16)
    b2 = jnp.concatenate([hd(gla_b2_fwd), hd(gla_b2_bwd)], axis=1)[:, None, :]

    g_idx = jnp.arange(4)
    h_idx = jnp.arange(N_HEADS)
    src = SM_IF + 4 * g_idx[None, :] + h_idx[:, None]
    dst = jnp.concatenate([4 * r + g_idx for r in range(GATE_COPIES)] + [jnp.array([LANES + 1, LANES + 3])])
    gsrc = jnp.concatenate([g_idx] * GATE_COPIES + [jnp.array([0, 2])])
    selcol = jnp.zeros((N_HEADS, LANES, 2 * LANES), F32).at[h_idx[:, None], src[:, gsrc], dst[None, :]].set(1.0)
    bif = ml_b_if.reshape(4, N_HEADS).T
    bifcol = jnp.zeros((N_HEADS, 2 * LANES), F32).at[:, dst].set(bif[:, gsrc])[:, None, :]

    cw = jnp.concatenate([jnp.moveaxis(hd(ml_conv_w[:, :512]), 1, 0), jnp.moveaxis(hd(ml_conv_w[:, 512:]), 1, 0)],
                         axis=2)
    cb = jnp.concatenate([hd(ml_conv_b[:512]), hd(ml_conv_b[512:])], axis=1)[:, None, :]
    mp = dict(w2=w2, b2=b2, selcol=selcol.astype(BF16), bifcol=bifcol, cw=cw, cb=cb,
              gn=gla_norm.reshape(N_HEADS, 1, DV), mn=ml_norm.reshape(N_HEADS, 1, DV))
    return wm, ws, mp


def kernel(x_prompt, x_sample, g_ffn1, w_ffn1_in, w_ffn1_out, g_mix, w_in, gla_w2_fwd, gla_b2_fwd, gla_w2_bwd,
           gla_b2_bwd, gla_norm, ml_conv_w, ml_conv_b, ml_b_if, ml_norm, w_out, g_ffn2, w_ffn2_in, w_ffn2_out,
           g_final):
    depth = w_in.shape[0]
    layers = []
    for l in range(depth):
        layers.append(dict(
            g1=g_ffn1[l][None, :], w1i=w_ffn1_in[l].astype(BF16), w1o=w_ffn1_out[l].astype(BF16),
            gm=g_mix[l][None, :],
            mix=_prep_layer(w_in[l], gla_w2_fwd[l], gla_b2_fwd[l], gla_w2_bwd[l], gla_b2_bwd[l], gla_norm[l],
                            ml_conv_w[l], ml_conv_b[l], ml_b_if[l], ml_norm[l]),
            wo=w_out[l].astype(BF16), g2=g_ffn2[l][None, :], w2i=w_ffn2_in[l].astype(BF16),
            w2o=w_ffn2_out[l].astype(BF16)))
    gf = g_final[None, :]

    def trunk(x):
        bsz, seq, _ = x.shape
        for l, p in enumerate(layers):
            wm, ws, mp = p["mix"]
            x1 = _ffn1(x.reshape(bsz * seq, D_MODEL), p["g1"], p["w1i"], p["w1o"])
            um, sm = _inproj(x1.reshape(bsz, seq, D_MODEL), p["gm"], wm, ws)
            merged = _mixer(um, sm, mp)
            x = _outffn(x1, merged.reshape(bsz * seq, D_MODEL), p["wo"], p["g2"], p["w2i"], p["w2o"], gf,
                        final_norm=(l == depth - 1))
            x = x.reshape(bsz, seq, D_MODEL)
        return x

    return trunk(x_prompt), trunk(x_sample)
```

```python
import functools

import jax
import jax.numpy as jnp
from jax import lax
from jax.experimental import pallas as pl
from jax.experimental.pallas import tpu as pltpu

F32 = jnp.float32
BF16 = jnp.bfloat16

D_MODEL = 1024
D_FF = 2816
N_HEADS = 4
DK = 128
DV = 256
GLA_LOWRANK = 16
GLA_TAU = 16.0
EPS = 1e-6
QK_SCALE = DK ** -0.5
LOG2E = 1.4426950408889634

LANES = 128
SUBLANES = 8
CHUNK = 128
GLA_MID = CHUNK // 2
TM_FFN = 512
TF = 256
TN_IN = 512
PAR_UNROLL = 2
GATE_ROWS = 512
GLA_GROUP = 2
SCAN_GROUP = 2
PIPE_SLOTS = 2 * max(GLA_GROUP, SCAN_GROUP)
VMEM_LIMIT = 56 * 1024 * 1024

GROUPS = ("gqk", "gv", "gr", "mqk", "mv", "mo", "ga", "gb")
GROUP_ACT = dict(gqk=None, gv=None, gr="silu", mqk=None, mv=None, mo="sigmoid", ga="sigmoid", gb="sigmoid")
N_MAIN = len(GROUPS) * N_HEADS * DV
SM_LR_F, SM_LR_B, SM_IF = 0, GLA_LOWRANK, 2 * GLA_LOWRANK


def _dot(a, b):
    return jnp.dot(a, b, preferred_element_type=F32)


def _dot_nt(a, b):
    return lax.dot_general(a, b, (((1,), (1,)), ((), ())), preferred_element_type=F32)


def _rms(x, g):
    return x * lax.rsqrt(jnp.mean(x * x, axis=-1, keepdims=True) + EPS) * g


def _logsig(z):
    return jnp.minimum(z, 0.0) - jnp.log(1.0 + jnp.exp(-jnp.abs(z)))


def _split_hi_lo(x):
    hi = x.astype(BF16)
    lo = (x - hi.astype(F32)).astype(BF16)
    return hi, lo


def _sel_dot(sel_bf16, x):
    hi, lo = _split_hi_lo(x)
    return _dot(sel_bf16, hi) + _dot(sel_bf16, lo)


def _dot_sel(x, sel_bf16):
    hi, lo = _split_hi_lo(x)
    return _dot(hi, sel_bf16) + _dot(lo, sel_bf16)


def _tile(x):
    return jnp.broadcast_to(x, (SUBLANES, LANES))


def _swiglu_into(h_scr, win_ref, wout_ref, act_scr):
    for j in range(D_FF // TF):
        a = _dot(h_scr[...], win_ref[:, j * TF:(j + 1) * TF])
        g = _dot(h_scr[...], win_ref[:, D_FF + j * TF:D_FF + (j + 1) * TF])
        act_scr[:, j * TF:(j + 1) * TF] = (a * jax.nn.sigmoid(a) * g).astype(BF16)
    return _dot(act_scr[...], wout_ref[...])


def _ffn1_kernel(x_ref, g_ref, win_ref, wout_ref, o_ref, h_scr, act_scr):
    x = x_ref[...]
    h_scr[...] = _rms(x, g_ref[...]).astype(BF16)
    o_ref[...] = x + 0.5 * _swiglu_into(h_scr, win_ref, wout_ref, act_scr)


def _outffn_kernel(x_ref, m_ref, wo_ref, g_ref, win_ref, wout_ref, gf_ref, o_ref, h_scr, act_scr, *, final_norm):
    x = x_ref[...] + _dot(m_ref[...], wo_ref[...])
    h_scr[...] = _rms(x, g_ref[...]).astype(BF16)
    x = x + 0.5 * _swiglu_into(h_scr, win_ref, wout_ref, act_scr)
    o_ref[...] = _rms(x, gf_ref[...]) if final_norm else x


def _inproj_kernel(x_ref, g_ref, wm_ref, ws_ref, um_ref, sm_ref, h_scr):
    h_scr[...] = _rms(x_ref[0], g_ref[...]).astype(BF16)
    for j in range(N_MAIN // TN_IN):
        cols = slice(j * TN_IN, (j + 1) * TN_IN)
        u = _dot(h_scr[...], wm_ref[:, cols])
        act = GROUP_ACT[GROUPS[j * TN_IN // (N_HEADS * DV)]]
        if act == "silu":
            u = u * jax.nn.sigmoid(u)
        elif act == "sigmoid":
            u = jax.nn.sigmoid(u)
        um_ref[0, :, cols] = u.astype(BF16)
    sm_ref[0] = _dot(h_scr[...], ws_ref[...])


def _mixer_kernel(gqk_ref, gv_ref, gr_ref, mqk_ref, mv_ref, mo_ref, ga_ref, gb_ref, sm_ref,
                  w2_ref, b2_ref, selrow_ref, bifrow_ref, cw_ref, cb_ref, gn_ref, mn_ref,
                  out_ref,
                  mqk_scr, bcum_scr, att_scr, qin_scr, kst_t_scr, dcol_scr, gt_scr, grow_scr, bt_scr, qkt_scr, kt_scr,
                  a_scr, g_scr, mprev_scr, mnew_scr,
                  sml_scr, qw_scr, ks_t_scr, floor_scr, dint_scr, dec_scr, dn_scr, og_scr, hm_scr, s_scr, c_scr, n_scr):
    seq = gqk_ref.shape[1]
    nch = seq // CHUNK
    ri = lax.broadcasted_iota(jnp.int32, (CHUNK, CHUNK), 0)
    ci = lax.broadcasted_iota(jnp.int32, (CHUNK, CHUNK), 1)
    lower = ci <= ri
    upper = ci >= ri
    eye = ci == ri
    lower_bf = lower.astype(BF16)
    upper_bf = upper.astype(BF16)
    dirs = ((lower, GLA_MID - 1, CHUNK - 1), (upper, GLA_MID, 0))

    def chunk_rows(c):
        return pl.ds(pl.multiple_of(c * CHUNK, CHUNK), CHUNK)

    def to_col(row):
        return jnp.sum(jnp.where(eye, row, 0.0), axis=1, keepdims=True)

    x = mqk_ref[0].astype(F32)
    cid = lax.broadcasted_iota(jnp.int32, (1, 2 * DK), 1)
    q_scale = jnp.where(cid < DK, QK_SCALE, 1.0)
    prev = pltpu.roll(x, 1, 0)
    nxt = pltpu.roll(x, seq - 1, 0)
    cw = cw_ref[0]

    def conv_silu(p, xc, n):
        cv = cb_ref[0] + p * cw[0:1] + xc * cw[1:2] + n * cw[2:3]
        return cv * jax.nn.sigmoid(cv) * q_scale

    mqk_scr[...] = conv_silu(prev, x, nxt)
    r8 = lax.broadcasted_iota(jnp.int32, (SUBLANES, 2 * DK), 0)
    top = slice(0, SUBLANES)
    bot = slice(seq - SUBLANES, seq)
    mqk_scr[top, :] = conv_silu(jnp.where(r8 == 0, 0.0, prev[top]), x[top], nxt[top])
    mqk_scr[bot, :] = conv_silu(prev[bot], x[bot], jnp.where(r8 == SUBLANES - 1, 0.0, nxt[bot]))

    def act_body(j, carry):
        rows = pl.ds(pl.multiple_of(j * GATE_ROWS, GATE_ROWS), GATE_ROWS)
        sm = sm_ref[0, rows, :]
        hm_scr[0, rows, :] = _logsig(_dot(sm.astype(BF16), w2_ref[0]) + b2_ref[0]) * (LOG2E / GLA_TAU)
        hi, lo = _split_hi_lo(sm)
        g_t = _dot_nt(selrow_ref[0], hi) + _dot_nt(selrow_ref[0], lo) + bifrow_ref[0]
        is_forget = lax.broadcasted_iota(jnp.int32, (SUBLANES, GATE_ROWS), 0) % 2 == 1
        g_t = jnp.where(is_forget, _logsig(g_t), g_t) * LOG2E
        for jj in range(GATE_ROWS // CHUNK):
            gt_scr[j * (GATE_ROWS // CHUNK) + jj] = g_t[:, jj * CHUNK:(jj + 1) * CHUNK]
        return carry

    lax.fori_loop(0, seq // GATE_ROWS, act_body, 0)

    row8 = lax.broadcasted_iota(jnp.int32, (SUBLANES, CHUNK), 0)

    def gate_chunk(c, slot):
        rows = chunk_rows(c)
        bcum_scr[slot, 0] = _sel_dot(lower_bf, hm_scr[0, rows, :DK])
        bcum_scr[slot, 1] = _sel_dot(upper_bf, hm_scr[0, rows, DK:])
        r = gt_scr[c]
        cum_f = _dot_sel(r, upper_bf)
        cum_b = _dot_sel(r, lower_bf)
        i_f, b_f, i_b, b_b = r[0:1], cum_f[1:2], r[2:3], cum_b[3:4]
        c_f = i_f - b_f
        c_b = i_b - b_b
        g8 = jnp.zeros((SUBLANES, CHUNK), F32)
        for idx, val in enumerate((i_f, b_f, i_b, b_b, c_f, c_b)):
            g8 = jnp.where(row8 == idx, val, g8)
        grow_scr[c] = g8
        for d, b, cc in ((0, b_f, c_f), (1, b_b, c_b)):
            l_idx = dirs[d][2]
            a = b[:, l_idx:l_idx + 1]
            a_scr[d, c] = _tile(a)
            g_scr[d, c] = _tile(a + jnp.max(cc, axis=1, keepdims=True))
            bt_scr[d, rows, :] = jnp.broadcast_to(to_col(b), (CHUNK, LANES))

    def gla_chunk(c, slot):
        rows = chunk_rows(c)
        qk = gqk_ref[0, rows, :].astype(F32)
        q = qk[:, :DK] * QK_SCALE
        k = qk[:, DK:]
        for d in (0, 1):
            mask, r_idx, l_idx = dirs[d]
            b = bcum_scr[slot, d]
            rho = b[r_idx:r_idx + 1, :]
            b_last = b[l_idx:l_idx + 1, :]
            q_mid = (q * jnp.exp2(b - rho)).astype(BF16)
            k_mid = (k * jnp.exp2(rho - b)).astype(BF16)
            att_scr[d, rows, :] = jnp.where(mask, _dot_nt(q_mid, k_mid), 0.0).astype(BF16)
            qin_scr[d, rows, :] = (q * jnp.exp2(b)).astype(BF16)
            kst_t_scr[d, c] = (k * jnp.exp2(b_last - b)).astype(BF16).T
            dcol_scr[d, c] = jnp.broadcast_to(to_col(jnp.exp2(b_last)), (DK, LANES))
        mqk = mqk_scr[rows, :].astype(BF16)
        qkt_scr[rows, :] = _dot_nt(mqk[:, :DK], mqk[:, DK:])
        kt_scr[c] = mqk[:, DK:].T

    def pipelined(produce, consume, group):
        for j in range(group):
            produce(j, j)

        def body(q, carry):
            for half in (0, 1):
                base = (2 * q + half) * group
                for j in range(group):
                    produce(jnp.minimum(base + group + j, nch - 1), (1 - half) * group + j)
                for j in range(group):
                    consume(base + j, half * group + j)
            return carry

        lax.fori_loop(0, nch // (2 * group), body, 0)

    pipelined(gate_chunk, gla_chunk, GLA_GROUP)

    def m_body(i, carry):
        m_f, m_b = carry
        cb = nch - 1 - i
        mprev_scr[0, i] = m_f
        mprev_scr[1, cb] = m_b
        m_f = jnp.maximum(a_scr[0, i] + m_f, g_scr[0, i])
        m_b = jnp.maximum(a_scr[1, cb] + m_b, g_scr[1, cb])
        mnew_scr[0, i] = m_f
        mnew_scr[1, cb] = m_b
        return m_f, m_b

    m0 = jnp.full((SUBLANES, LANES), -jnp.inf, F32)
    lax.fori_loop(0, nch, m_body, (m0, m0))

    def prep_dir(d, c, slot):
        rows = chunk_rows(c)
        mask = dirs[d][0]
        g8 = grow_scr[c]
        b_t = bt_scr[d, rows, :]
        c_s = g8[4 + d:5 + d, :]
        m_prev = mprev_scr[d, c][0:1, 0:1]
        m_new = mnew_scr[d, c][0:1, 0:1]
        b_last = a_scr[d, c][0:1, 0:1]
        log_d = jnp.where(mask, b_t + c_s, -jnp.inf)
        m_inter = b_t + m_prev
        m_t = jnp.maximum(m_inter, jnp.max(log_d, axis=1, keepdims=True))
        sml = qkt_scr[rows, :] * jnp.exp2(log_d - m_t)
        sml_scr[slot, d] = sml.astype(BF16)
        dint_scr[slot, d] = jnp.broadcast_to(jnp.sum(sml, axis=1, keepdims=True), (CHUNK, LANES))
        qw_scr[slot, d] = (jnp.exp2(m_inter - m_t) * mqk_scr[rows, :DK]).astype(BF16)
        floor_scr[slot, d] = jnp.exp2(-m_t)
        w_st = jnp.broadcast_to(jnp.exp2(b_last + c_s - m_new), (SUBLANES, CHUNK))
        ks_t_scr[slot, d] = (kt_scr[c].astype(F32) * w_st[0:1, :]).astype(BF16)
        dn_scr[slot, d] = _dot(w_st.astype(BF16), mqk_scr[rows, DK:].astype(BF16))
        dec_scr[slot, d] = _tile(jnp.exp2(b_last + m_prev - m_new))

    def scan_dir(d, c, slot):
        rows = chunk_rows(c)
        v = gv_ref[0, rows, :]
        s = s_scr[d]
        og_scr[d, rows, :] = _dot(jnp.concatenate([att_scr[d, rows, :], qin_scr[d, rows, :]], axis=1),
                                  jnp.concatenate([v, s.astype(BF16)], axis=0))
        dcol = dcol_scr[d, c]
        s_scr[d] = s * jnp.concatenate([dcol, dcol], axis=1) + _dot(kst_t_scr[d, c], v)
        mv = mv_ref[0, rows, :]
        cst = c_scr[d]
        nst = n_scr[d]
        qw = qw_scr[slot, d]
        num = _dot(jnp.concatenate([sml_scr[slot, d], qw], axis=1), jnp.concatenate([mv, cst.astype(BF16)], axis=0))
        den = dint_scr[slot, d] + jnp.sum(qw.astype(F32) * nst[0:1, :], axis=1, keepdims=True)
        rden = 1.0 / jnp.maximum(jnp.abs(den), floor_scr[slot, d])
        hm_scr[d, rows, :] = num * jnp.concatenate([rden, rden], axis=1)
        decay = dec_scr[slot, d][0:1, 0:1]
        c_scr[d] = decay * cst + _dot(ks_t_scr[slot, d], mv)
        n_scr[d] = decay * nst + dn_scr[slot, d]

    s_scr[...] = jnp.zeros_like(s_scr)
    c_scr[...] = jnp.zeros_like(c_scr)
    n_scr[...] = jnp.zeros_like(n_scr)

    def prep_step(i, slot):
        prep_dir(0, i, slot)
        prep_dir(1, nch - 1 - i, slot)

    def scan_step(i, slot):
        scan_dir(0, i, slot)
        scan_dir(1, nch - 1 - i, slot)

    pipelined(prep_step, scan_step, SCAN_GROUP)

    def merge_body(c, carry):
        rows = chunk_rows(c)
        og = og_scr[0, rows, :] + og_scr[1, rows, :]
        hm = hm_scr[0, rows, :] + hm_scr[1, rows, :]
        o_a = (og * lax.rsqrt(jnp.mean(og * og, axis=-1, keepdims=True) + EPS) * gn_ref[0]) \
            * gr_ref[0, rows, :].astype(F32)
        o_b = (hm * lax.rsqrt(jnp.mean(hm * hm, axis=-1, keepdims=True) + EPS) * mn_ref[0]) \
            * mo_ref[0, rows, :].astype(F32)
        merged = ga_ref[0, rows, :].astype(F32) * o_a + gb_ref[0, rows, :].astype(F32) * o_b
        out_ref[0, rows, :] = merged.astype(BF16)
        return carry

    lax.fori_loop(0, nch, merge_body, 0, unroll=PAR_UNROLL)


def _resident(shape):
    nd = len(shape)
    return pl.BlockSpec(shape, lambda *_: (0,) * nd, pipeline_mode=pl.Buffered(1))


def _params(n_axes):
    return pltpu.CompilerParams(dimension_semantics=("arbitrary",) * n_axes, vmem_limit_bytes=VMEM_LIMIT)


def _ffn1(x2d, g, win, wout):
    m = x2d.shape[0]
    tile = pl.BlockSpec((TM_FFN, D_MODEL), lambda i: (i, 0))
    return pl.pallas_call(
        _ffn1_kernel,
        grid=(m // TM_FFN,),
        in_specs=[tile, _resident(g.shape), _resident(win.shape), _resident(wout.shape)],
        out_specs=tile,
        out_shape=jax.ShapeDtypeStruct((m, D_MODEL), F32),
        scratch_shapes=[pltpu.VMEM((TM_FFN, D_MODEL), BF16), pltpu.VMEM((TM_FFN, D_FF), BF16)],
        compiler_params=_params(1),
        name="ffn1",
    )(x2d, g, win, wout)


def _outffn(x2d, merged2d, wo, g, win, wout, gf, final_norm):
    m = x2d.shape[0]
    tile = pl.BlockSpec((TM_FFN, D_MODEL), lambda i: (i, 0))
    return pl.pallas_call(
        functools.partial(_outffn_kernel, final_norm=final_norm),
        grid=(m // TM_FFN,),
        in_specs=[tile, tile, _resident(wo.shape), _resident(g.shape), _resident(win.shape), _resident(wout.shape),
                  _resident(gf.shape)],
        out_specs=tile,
        out_shape=jax.ShapeDtypeStruct((m, D_MODEL), F32),
        scratch_shapes=[pltpu.VMEM((TM_FFN, D_MODEL), BF16), pltpu.VMEM((TM_FFN, D_FF), BF16)],
        compiler_params=_params(1),
        name="outffn",
    )(x2d, merged2d, wo, g, win, wout, gf)


def _inproj(x, g, wm, ws):
    bsz, seq, _ = x.shape
    tm = TM_FFN
    return pl.pallas_call(
        _inproj_kernel,
        grid=(bsz, seq // tm),
        in_specs=[pl.BlockSpec((1, tm, D_MODEL), lambda b, i: (b, i, 0)),
                  _resident(g.shape), _resident(wm.shape), _resident(ws.shape)],
        out_specs=[pl.BlockSpec((1, tm, N_MAIN), lambda b, i: (b, i, 0)),
                   pl.BlockSpec((1, tm, LANES), lambda b, i: (b, i, 0))],
        out_shape=[jax.ShapeDtypeStruct((bsz, seq, N_MAIN), BF16),
                   jax.ShapeDtypeStruct((bsz, seq, LANES), F32)],
        scratch_shapes=[pltpu.VMEM((tm, D_MODEL), BF16)],
        compiler_params=_params(2),
        name="inproj",
    )(x, g, wm, ws)


def _mixer(um, sm, mp):
    bsz, seq, _ = um.shape
    nch = seq // CHUNK

    def group(name):
        base = GROUPS.index(name) * N_HEADS
        return pl.BlockSpec((1, seq, DV), lambda b, h: (b, 0, base + h))

    def head(arr):
        nd = arr.ndim
        return pl.BlockSpec((1,) + arr.shape[1:], lambda b, h: (h,) + (0,) * (nd - 1))

    plist = [mp[n] for n in ("w2", "b2", "selrow", "bifrow", "cw", "cb", "gn", "mn")]
    in_specs = [group(n) for n in GROUPS] + [pl.BlockSpec((1, seq, LANES), lambda b, h: (b, 0, 0))] \
        + [head(p) for p in plist]
    tok_bf = pltpu.VMEM((2, seq, DK), BF16)
    chunk_t_bf = pltpu.VMEM((2, nch, DK, CHUNK), BF16)
    chunk_tile = pltpu.VMEM((2, nch, SUBLANES, LANES), F32)
    slot_bf = pltpu.VMEM((PIPE_SLOTS, 2, CHUNK, CHUNK), BF16)
    slot_f32 = pltpu.VMEM((PIPE_SLOTS, 2, CHUNK, LANES), F32)
    slot_tile = pltpu.VMEM((PIPE_SLOTS, 2, SUBLANES, LANES), F32)
    return pl.pallas_call(
        _mixer_kernel,
        grid=(bsz, N_HEADS),
        in_specs=in_specs,
        out_specs=pl.BlockSpec((1, seq, DV), lambda b, h: (b, 0, h)),
        out_shape=jax.ShapeDtypeStruct((bsz, seq, N_HEADS * DV), BF16),
        scratch_shapes=[pltpu.VMEM((seq, 2 * DK), F32),
                        slot_f32,
                        tok_bf, tok_bf, chunk_t_bf,
                        pltpu.VMEM((2, nch, DK, LANES), F32),
                        pltpu.VMEM((nch, SUBLANES, CHUNK), F32),
                        pltpu.VMEM((nch, SUBLANES, CHUNK), F32),
                        pltpu.VMEM((2, seq, LANES), F32),
                        pltpu.VMEM((seq, CHUNK), F32),
                        pltpu.VMEM((nch, DK, CHUNK), BF16),
                        chunk_tile, chunk_tile, chunk_tile, chunk_tile,
                        slot_bf, slot_bf, slot_bf,
                        slot_f32, slot_f32,
                        slot_tile, slot_tile,
                        pltpu.VMEM((2, seq, DV), F32), pltpu.VMEM((2, seq, DV), F32),
                        pltpu.VMEM((2, DK, DV), F32), pltpu.VMEM((2, DK, DV), F32),
                        pltpu.VMEM((2, SUBLANES, DK), F32)],
        compiler_params=_params(2),
        name="mixer",
    )(*([um] * len(GROUPS)), sm, *plist)


def _prep_layer(w_in, gla_w2_fwd, gla_b2_fwd, gla_w2_bwd, gla_b2_bwd, gla_norm, ml_conv_w, ml_conv_b, ml_b_if,
                ml_norm):
    sizes = (512, 512, 1024, 1024, 32, 1024, 1024, 1024, 16, 1024, 1024)
    offs = [0]
    for s in sizes:
        offs.append(offs[-1] + s)
    gq, gk, gv, gr, glr, mqk, mv, mo, mif, gate_a, gate_b = [w_in[:, offs[i]:offs[i + 1]] for i in range(len(sizes))]

    def pair(a, b):
        hd = lambda w: w.reshape(D_MODEL, N_HEADS, DK)
        return jnp.concatenate([hd(a), hd(b)], axis=2).reshape(D_MODEL, N_HEADS * 2 * DK)

    parts = dict(gqk=pair(gq, gk), gv=gv, gr=gr, mqk=pair(mqk[:, :512], mqk[:, 512:]), mv=mv, mo=mo, ga=gate_a,
                 gb=gate_b)
    wm = jnp.concatenate([parts[n] for n in GROUPS], axis=1).astype(BF16)
    ws = jnp.concatenate([glr, mif, jnp.zeros((D_MODEL, LANES - 48), F32)], axis=1).astype(BF16)

    hd = lambda a: a.reshape(a.shape[:-1] + (N_HEADS, DK))

    def w2pad(w2, row0):
        w = jnp.moveaxis(hd(w2), 1, 0)
        return jnp.pad(w, ((0, 0), (row0, LANES - row0 - GLA_LOWRANK), (0, 0)))

    w2 = jnp.concatenate([w2pad(gla_w2_fwd, SM_LR_F), w2pad(gla_w2_bwd, SM_LR_B)], axis=2).astype(BF16)
    b2 = jnp.concatenate([hd(gla_b2_fwd), hd(gla_b2_bwd)], axis=1)[:, None, :]

    g_idx = jnp.arange(4)
    h_idx = jnp.arange(N_HEADS)
    src = SM_IF + 4 * g_idx[None, :] + h_idx[:, None]
    selrow = jnp.zeros((N_HEADS, SUBLANES, LANES), F32).at[h_idx[:, None], g_idx[None, :], src].set(1.0)
    bif = jnp.pad(ml_b_if.reshape(4, N_HEADS).T, ((0, 0), (0, SUBLANES - 4)))
    bifrow = jnp.broadcast_to(bif[:, :, None], (N_HEADS, SUBLANES, GATE_ROWS))

    cw = jnp.concatenate([jnp.moveaxis(hd(ml_conv_w[:, :512]), 1, 0), jnp.moveaxis(hd(ml_conv_w[:, 512:]), 1, 0)],
                         axis=2)
    cb = jnp.concatenate([hd(ml_conv_b[:512]), hd(ml_conv_b[512:])], axis=1)[:, None, :]
    mp = dict(w2=w2, b2=b2, selrow=selrow.astype(BF16), bifrow=bifrow, cw=cw, cb=cb,
              gn=gla_norm.reshape(N_HEADS, 1, DV), mn=ml_norm.reshape(N_HEADS, 1, DV))
    return wm, ws, mp


def kernel(x_prompt, x_sample, g_ffn1, w_ffn1_in, w_ffn1_out, g_mix, w_in, gla_w2_fwd, gla_b2_fwd, gla_w2_bwd,
           gla_b2_bwd, gla_norm, ml_conv_w, ml_conv_b, ml_b_if, ml_norm, w_out, g_ffn2, w_ffn2_in, w_ffn2_out,
           g_final):
    depth = w_in.shape[0]
    layers = []
    for l in range(depth):
        layers.append(dict(
            g1=g_ffn1[l][None, :], w1i=w_ffn1_in[l].astype(BF16), w1o=w_ffn1_out[l].astype(BF16),
            gm=g_mix[l][None, :],
            mix=_prep_layer(w_in[l], gla_w2_fwd[l], gla_b2_fwd[l], gla_w2_bwd[l], gla_b2_bwd[l], gla_norm[l],
                            ml_conv_w[l], ml_conv_b[l], ml_b_if[l], ml_norm[l]),
            wo=w_out[l].astype(BF16), g2=g_ffn2[l][None, :], w2i=w_ffn2_in[l].astype(BF16),
            w2o=w_ffn2_out[l].astype(BF16)))
    gf = g_final[None, :]

    def trunk(x):
        bsz, seq, _ = x.shape
        for l, p in enumerate(layers):
            wm, ws, mp = p["mix"]
            x1 = _ffn1(x.reshape(bsz * seq, D_MODEL), p["g1"], p["w1i"], p["w1o"])
            um, sm = _inproj(x1.reshape(bsz, seq, D_MODEL), p["gm"], wm, ws)
            merged = _mixer(um, sm, mp)
            x = _outffn(x1, merged.reshape(bsz * seq, D_MODEL), p["wo"], p["g2"], p["w2i"], p["w2o"], gf,
                        final_norm=(l == depth - 1))
            x = x.reshape(bsz, seq, D_MODEL)
        return x

    return trunk(x_prompt), trunk(x_sample)
```

```python
import functools

import jax
import jax.numpy as jnp
from jax import lax
from jax.experimental import pallas as pl
from jax.experimental.pallas import tpu as pltpu

F32 = jnp.float32
BF16 = jnp.bfloat16

D_MODEL = 1024
D_FF = 2816
N_HEADS = 4
DK = 128
DV = 256
GLA_LOWRANK = 16
GLA_TAU = 16.0
EPS = 1e-6
QK_SCALE = DK ** -0.5
LOG2E = 1.4426950408889634

LANES = 128
SUBLANES = 8
CHUNK = 128
GLA_MID = CHUNK // 2
TM_FFN = 512
TF = 256
TN_IN = 512
PAR_UNROLL = 2
GATE_ROWS = 512
GLA_GROUP = 2
SCAN_GROUP = 2
PIPE_SLOTS = 2 * max(GLA_GROUP, SCAN_GROUP)
VMEM_LIMIT = 56 * 1024 * 1024

GROUP_W = N_HEADS * DV
GROUPS = ("gqk", "gv", "mqk", "mv", "wa", "wb")
PLAIN_GROUPS = 4
W_GATES = ("gr", "ga", "mo", "gb")
N_MAIN = len(GROUPS) * GROUP_W
SM_LR_F, SM_LR_B, SM_IF = 0, GLA_LOWRANK, 2 * GLA_LOWRANK


def _dot(a, b):
    return jnp.dot(a, b, preferred_element_type=F32)


def _dot_nt(a, b):
    return lax.dot_general(a, b, (((1,), (1,)), ((), ())), preferred_element_type=F32)


def _rms(x, g):
    return x * lax.rsqrt(jnp.mean(x * x, axis=-1, keepdims=True) + EPS) * g


def _logsig(z):
    return jnp.minimum(z, 0.0) - jnp.log(1.0 + jnp.exp(-jnp.abs(z)))


def _split_hi_lo(x):
    hi = x.astype(BF16)
    lo = (x - hi.astype(F32)).astype(BF16)
    return hi, lo


def _sel_dot(sel_bf16, x):
    hi, lo = _split_hi_lo(x)
    return _dot(sel_bf16, hi) + _dot(sel_bf16, lo)


def _dot_sel(x, sel_bf16):
    hi, lo = _split_hi_lo(x)
    return _dot(hi, sel_bf16) + _dot(lo, sel_bf16)


def _tile(x):
    return jnp.broadcast_to(x, (SUBLANES, LANES))


def _swiglu_into(h_scr, win_ref, wout_ref, act_scr):
    for j in range(D_FF // TF):
        a = _dot(h_scr[...], win_ref[:, j * TF:(j + 1) * TF])
        g = _dot(h_scr[...], win_ref[:, D_FF + j * TF:D_FF + (j + 1) * TF])
        act_scr[:, j * TF:(j + 1) * TF] = (a * jax.nn.sigmoid(a) * g).astype(BF16)
    return _dot(act_scr[...], wout_ref[...])


def _ffn1_kernel(x_ref, g_ref, win_ref, wout_ref, o_ref, h_scr, act_scr):
    x = x_ref[...]
    h_scr[...] = _rms(x, g_ref[...]).astype(BF16)
    o_ref[...] = x + 0.5 * _swiglu_into(h_scr, win_ref, wout_ref, act_scr)


def _outffn_kernel(x_ref, m_ref, wo_ref, g_ref, win_ref, wout_ref, gf_ref, o_ref, h_scr, act_scr, *, final_norm):
    x = x_ref[...] + _dot(m_ref[...], wo_ref[...])
    h_scr[...] = _rms(x, g_ref[...]).astype(BF16)
    x = x + 0.5 * _swiglu_into(h_scr, win_ref, wout_ref, act_scr)
    o_ref[...] = _rms(x, gf_ref[...]) if final_norm else x


def _inproj_kernel(x_ref, xp_ref, xn_ref, g_ref, wm_ref, ws_ref, nrm_ref, cw_ref, cb_ref, um_ref, sm_ref, h_scr):
    tm = x_ref.shape[1]
    h_scr[...] = _rms(x_ref[0], g_ref[...]).astype(BF16)
    i = pl.program_id(1)
    r16 = lax.broadcasted_iota(jnp.int32, (2 * SUBLANES, 1), 0)
    inside = jnp.where(r16 < SUBLANES, jnp.where(i > 0, 1.0, 0.0), jnp.where(i < pl.num_programs(1) - 1, 1.0, 0.0))
    h_halo = _rms(jnp.concatenate([xp_ref[0], xn_ref[0]], axis=0), g_ref[...]).astype(BF16)
    lane_q = lax.broadcasted_iota(jnp.int32, (1, TN_IN), 1) % (2 * DK) < DK
    q_scale = jnp.where(lane_q, QK_SCALE, 1.0)
    def plain(name, j):
        cols = slice(GROUPS.index(name) * GROUP_W + j * TN_IN, GROUPS.index(name) * GROUP_W + (j + 1) * TN_IN)
        um_ref[0, :, cols] = _dot(h_scr[...], wm_ref[:, cols]).astype(BF16)

    def conv(name, j):
        cols = slice(GROUPS.index(name) * GROUP_W + j * TN_IN, GROUPS.index(name) * GROUP_W + (j + 1) * TN_IN)
        cc = slice(j * TN_IN, (j + 1) * TN_IN)
        u = _dot(h_scr[...], wm_ref[:, cols])
        halo = _dot(h_halo, wm_ref[:, cols]) * inside
        xe = jnp.concatenate([halo[:SUBLANES], u, halo[SUBLANES:]], axis=0)
        prev = pltpu.roll(xe, 1, 0)[SUBLANES:SUBLANES + tm]
        nxt = pltpu.roll(xe, tm + 2 * SUBLANES - 1, 0)[SUBLANES:SUBLANES + tm]
        cv = cb_ref[:, cc] + prev * cw_ref[0:1, cc] + u * cw_ref[1:2, cc] + nxt * cw_ref[2:3, cc]
        um_ref[0, :, cols] = (cv * jax.nn.sigmoid(cv) * q_scale).astype(BF16)

    def gates(br, j):
        out_gate, branch_gate = ((PLAIN_GROUPS + 2 * br + t) * GROUP_W + j * TN_IN for t in (0, 1))
        u = _dot(h_scr[...], wm_ref[:, out_gate:out_gate + TN_IN])
        out_act = u * jax.nn.sigmoid(u) if br == 0 else jax.nn.sigmoid(u)
        w = out_act * jax.nn.sigmoid(_dot(h_scr[...], wm_ref[:, branch_gate:branch_gate + TN_IN]))
        dst = (PLAIN_GROUPS + br) * GROUP_W + j * TN_IN
        um_ref[0, :, dst:dst + TN_IN] = (w * nrm_ref[br:br + 1, j * TN_IN:(j + 1) * TN_IN]).astype(BF16)

    for heavy, light in zip(((conv, "mqk", 0), (gates, 0, 0), (conv, "mqk", 1), (gates, 0, 1), (gates, 1, 0),
                             (gates, 1, 1)),
                            (("gqk", 0), ("gqk", 1), ("gv", 0), ("gv", 1), ("mv", 0), ("mv", 1))):
        heavy[0](*heavy[1:])
        plain(*light)
    sm_ref[0] = _dot(h_scr[...], ws_ref[...])


def _mixer_kernel(gqk_ref, gv_ref, mqk_ref, mv_ref, wa_ref, wb_ref, sm_ref,
                  w2_ref, b2_ref, selrow_ref, bifrow_ref,
                  out_ref,
                  bcum_scr, att_scr, qin_scr, kst_t_scr, dcol_scr, gt_scr, grow_scr, bt_scr, qkt_scr, kt_scr,
                  a_scr, g_scr, mprev_scr, mnew_scr,
                  sml_scr, qw_scr, ks_t_scr, floor_scr, dint_scr, dec_scr, dn_scr, og_scr, hm_scr, s_scr, c_scr, n_scr):
    seq = gqk_ref.shape[1]
    nch = seq // CHUNK
    ri = lax.broadcasted_iota(jnp.int32, (CHUNK, CHUNK), 0)
    ci = lax.broadcasted_iota(jnp.int32, (CHUNK, CHUNK), 1)
    lower = ci <= ri
    upper = ci >= ri
    eye = ci == ri
    lower_bf = lower.astype(BF16)
    upper_bf = upper.astype(BF16)
    dirs = ((lower, GLA_MID - 1, CHUNK - 1), (upper, GLA_MID, 0))

    def chunk_rows(c):
        return pl.ds(pl.multiple_of(c * CHUNK, CHUNK), CHUNK)

    def to_col(row):
        return jnp.sum(jnp.where(eye, row, 0.0), axis=1, keepdims=True)

    def act_body(j, carry):
        rows = pl.ds(pl.multiple_of(j * GATE_ROWS, GATE_ROWS), GATE_ROWS)
        sm = sm_ref[0, rows, :]
        hm_scr[0, rows, :] = _logsig(_dot(sm.astype(BF16), w2_ref[0]) + b2_ref[0]) * (LOG2E / GLA_TAU)
        hi, lo = _split_hi_lo(sm)
        g_t = _dot_nt(selrow_ref[0], hi) + _dot_nt(selrow_ref[0], lo) + bifrow_ref[0]
        is_forget = lax.broadcasted_iota(jnp.int32, (SUBLANES, GATE_ROWS), 0) % 2 == 1
        g_t = jnp.where(is_forget, _logsig(g_t), g_t) * LOG2E
        for jj in range(GATE_ROWS // CHUNK):
            gt_scr[j * (GATE_ROWS // CHUNK) + jj] = g_t[:, jj * CHUNK:(jj + 1) * CHUNK]
        return carry

    lax.fori_loop(0, seq // GATE_ROWS, act_body, 0)

    row8 = lax.broadcasted_iota(jnp.int32, (SUBLANES, CHUNK), 0)

    def gate_chunk(c, slot):
        rows = chunk_rows(c)
        bcum_scr[slot, 0] = _sel_dot(lower_bf, hm_scr[0, rows, :DK])
        bcum_scr[slot, 1] = _sel_dot(upper_bf, hm_scr[0, rows, DK:])
        r = gt_scr[c]
        cum_f = _dot_sel(r, upper_bf)
        cum_b = _dot_sel(r, lower_bf)
        i_f, b_f, i_b, b_b = r[0:1], cum_f[1:2], r[2:3], cum_b[3:4]
        c_f = i_f - b_f
        c_b = i_b - b_b
        g8 = jnp.zeros((SUBLANES, CHUNK), F32)
        for idx, val in enumerate((i_f, b_f, i_b, b_b, c_f, c_b)):
            g8 = jnp.where(row8 == idx, val, g8)
        grow_scr[c] = g8
        for d, b, cc in ((0, b_f, c_f), (1, b_b, c_b)):
            l_idx = dirs[d][2]
            a = b[:, l_idx:l_idx + 1]
            a_scr[d, c] = _tile(a)
            g_scr[d, c] = _tile(a + jnp.max(cc, axis=1, keepdims=True))
            bt_scr[d, rows, :] = jnp.broadcast_to(to_col(b), (CHUNK, LANES))

    def gla_chunk(c, slot):
        rows = chunk_rows(c)
        qk = gqk_ref[0, rows, :].astype(F32)
        q = qk[:, :DK] * QK_SCALE
        k = qk[:, DK:]
        for d in (0, 1):
            mask, r_idx, l_idx = dirs[d]
            b = bcum_scr[slot, d]
            rho = b[r_idx:r_idx + 1, :]
            b_last = b[l_idx:l_idx + 1, :]
            q_mid = (q * jnp.exp2(b - rho)).astype(BF16)
            k_mid = (k * jnp.exp2(rho - b)).astype(BF16)
            att_scr[d, rows, :] = jnp.where(mask, _dot_nt(q_mid, k_mid), 0.0).astype(BF16)
            qin_scr[d, rows, :] = (q * jnp.exp2(b)).astype(BF16)
            kst_t_scr[d, c] = (k * jnp.exp2(b_last - b)).astype(BF16).T
            dcol_scr[d, c] = jnp.broadcast_to(to_col(jnp.exp2(b_last)), (DK, LANES))
        mqk = mqk_ref[0, rows, :]
        qkt_scr[rows, :] = _dot_nt(mqk[:, :DK], mqk[:, DK:])
        kt_scr[c] = mqk[:, DK:].T

    def pipelined(produce, consume, group):
        for j in range(group):
            produce(j, j)

        def body(q, carry):
            for half in (0, 1):
                base = (2 * q + half) * group
                for j in range(group):
                    produce(jnp.minimum(base + group + j, nch - 1), (1 - half) * group + j)
                for j in range(group):
                    consume(base + j, half * group + j)
            return carry

        lax.fori_loop(0, nch // (2 * group), body, 0)

    pipelined(gate_chunk, gla_chunk, GLA_GROUP)

    def m_body(i, carry):
        m_f, m_b = carry
        cb = nch - 1 - i
        mprev_scr[0, i] = m_f
        mprev_scr[1, cb] = m_b
        m_f = jnp.maximum(a_scr[0, i] + m_f, g_scr[0, i])
        m_b = jnp.maximum(a_scr[1, cb] + m_b, g_scr[1, cb])
        mnew_scr[0, i] = m_f
        mnew_scr[1, cb] = m_b
        return m_f, m_b

    m0 = jnp.full((SUBLANES, LANES), -jnp.inf, F32)
    lax.fori_loop(0, nch, m_body, (m0, m0))

    def prep_dir(d, c, slot):
        rows = chunk_rows(c)
        mask = dirs[d][0]
        g8 = grow_scr[c]
        b_t = bt_scr[d, rows, :]
        c_s = g8[4 + d:5 + d, :]
        m_prev = mprev_scr[d, c][0:1, 0:1]
        m_new = mnew_scr[d, c][0:1, 0:1]
        b_last = a_scr[d, c][0:1, 0:1]
        log_d = jnp.where(mask, b_t + c_s, -jnp.inf)
        m_inter = b_t + m_prev
        m_t = jnp.maximum(m_inter, jnp.max(log_d, axis=1, keepdims=True))
        sml = qkt_scr[rows, :] * jnp.exp2(log_d - m_t)
        sml_scr[slot, d] = sml.astype(BF16)
        dint_scr[slot, d] = jnp.broadcast_to(jnp.sum(sml, axis=1, keepdims=True), (CHUNK, LANES))
        qw_scr[slot, d] = (jnp.exp2(m_inter - m_t) * mqk_ref[0, rows, :DK].astype(F32)).astype(BF16)
        floor_scr[slot, d] = jnp.exp2(-m_t)
        w_st = jnp.broadcast_to(jnp.exp2(b_last + c_s - m_new), (SUBLANES, CHUNK))
        ks_t_scr[slot, d] = (kt_scr[c].astype(F32) * w_st[0:1, :]).astype(BF16)
        dn_scr[slot, d] = _dot(w_st.astype(BF16), mqk_ref[0, rows, DK:])
        dec_scr[slot, d] = _tile(jnp.exp2(b_last + m_prev - m_new))

    def scan_dir(d, c, slot):
        rows = chunk_rows(c)
        v = gv_ref[0, rows, :]
        s = s_scr[d]
        og_scr[d, rows, :] = _dot(jnp.concatenate([att_scr[d, rows, :], qin_scr[d, rows, :]], axis=1),
                                  jnp.concatenate([v, s.astype(BF16)], axis=0))
        dcol = dcol_scr[d, c]
        s_scr[d] = s * jnp.concatenate([dcol, dcol], axis=1) + _dot(kst_t_scr[d, c], v)
        mv = mv_ref[0, rows, :]
        cst = c_scr[d]
        nst = n_scr[d]
        qw = qw_scr[slot, d]
        num = _dot(jnp.concatenate([sml_scr[slot, d], qw], axis=1), jnp.concatenate([mv, cst.astype(BF16)], axis=0))
        den = dint_scr[slot, d] + jnp.sum(qw.astype(F32) * nst[0:1, :], axis=1, keepdims=True)
        rden = 1.0 / jnp.maximum(jnp.abs(den), floor_scr[slot, d])
        hm_scr[d, rows, :] = num * jnp.concatenate([rden, rden], axis=1)
        decay = dec_scr[slot, d][0:1, 0:1]
        c_scr[d] = decay * cst + _dot(ks_t_scr[slot, d], mv)
        n_scr[d] = decay * nst + dn_scr[slot, d]

    s_scr[...] = jnp.zeros_like(s_scr)
    c_scr[...] = jnp.zeros_like(c_scr)
    n_scr[...] = jnp.zeros_like(n_scr)

    def prep_step(i, slot):
        prep_dir(0, i, slot)
        prep_dir(1, nch - 1 - i, slot)

    def scan_step(i, slot):
        scan_dir(0, i, slot)
        scan_dir(1, nch - 1 - i, slot)

    pipelined(prep_step, scan_step, SCAN_GROUP)

    def merge_body(c, carry):
        rows = chunk_rows(c)
        og = og_scr[0, rows, :] + og_scr[1, rows, :]
        hm = hm_scr[0, rows, :] + hm_scr[1, rows, :]
        o_a = og * lax.rsqrt(jnp.mean(og * og, axis=-1, keepdims=True) + EPS)
        o_b = hm * lax.rsqrt(jnp.mean(hm * hm, axis=-1, keepdims=True) + EPS)
        merged = o_a * wa_ref[0, rows, :].astype(F32) + o_b * wb_ref[0, rows, :].astype(F32)
        out_ref[0, rows, :] = merged.astype(BF16)
        return carry

    lax.fori_loop(0, nch, merge_body, 0, unroll=PAR_UNROLL)


def _resident(shape):
    nd = len(shape)
    return pl.BlockSpec(shape, lambda *_: (0,) * nd, pipeline_mode=pl.Buffered(1))


def _params(n_axes):
    return pltpu.CompilerParams(dimension_semantics=("arbitrary",) * n_axes, vmem_limit_bytes=VMEM_LIMIT)


def _ffn1(x2d, g, win, wout):
    m = x2d.shape[0]
    tile = pl.BlockSpec((TM_FFN, D_MODEL), lambda i: (i, 0))
    return pl.pallas_call(
        _ffn1_kernel,
        grid=(m // TM_FFN,),
        in_specs=[tile, _resident(g.shape), _resident(win.shape), _resident(wout.shape)],
        out_specs=tile,
        out_shape=jax.ShapeDtypeStruct((m, D_MODEL), F32),
        scratch_shapes=[pltpu.VMEM((TM_FFN, D_MODEL), BF16), pltpu.VMEM((TM_FFN, D_FF), BF16)],
        compiler_params=_params(1),
        name="ffn1",
    )(x2d, g, win, wout)


def _outffn(x2d, merged2d, wo, g, win, wout, gf, final_norm):
    m = x2d.shape[0]
    tile = pl.BlockSpec((TM_FFN, D_MODEL), lambda i: (i, 0))
    return pl.pallas_call(
        functools.partial(_outffn_kernel, final_norm=final_norm),
        grid=(m // TM_FFN,),
        in_specs=[tile, tile, _resident(wo.shape), _resident(g.shape), _resident(win.shape), _resident(wout.shape),
                  _resident(gf.shape)],
        out_specs=tile,
        out_shape=jax.ShapeDtypeStruct((m, D_MODEL), F32),
        scratch_shapes=[pltpu.VMEM((TM_FFN, D_MODEL), BF16), pltpu.VMEM((TM_FFN, D_FF), BF16)],
        compiler_params=_params(1),
        name="outffn",
    )(x2d, merged2d, wo, g, win, wout, gf)


def _inproj(x, g, wm, ws, nrm, cw, cb):
    bsz, seq, _ = x.shape
    tm = TM_FFN
    per_tile = tm // SUBLANES
    last = seq // SUBLANES - 1
    return pl.pallas_call(
        _inproj_kernel,
        grid=(bsz, seq // tm),
        in_specs=[pl.BlockSpec((1, tm, D_MODEL), lambda b, i: (b, i, 0)),
                  pl.BlockSpec((1, SUBLANES, D_MODEL), lambda b, i: (b, jnp.maximum(i * per_tile - 1, 0), 0)),
                  pl.BlockSpec((1, SUBLANES, D_MODEL), lambda b, i: (b, jnp.minimum((i + 1) * per_tile, last), 0)),
                  _resident(g.shape), _resident(wm.shape), _resident(ws.shape), _resident(nrm.shape),
                  _resident(cw.shape), _resident(cb.shape)],
        out_specs=[pl.BlockSpec((1, tm, N_MAIN), lambda b, i: (b, i, 0)),
                   pl.BlockSpec((1, tm, LANES), lambda b, i: (b, i, 0))],
        out_shape=[jax.ShapeDtypeStruct((bsz, seq, N_MAIN), BF16),
                   jax.ShapeDtypeStruct((bsz, seq, LANES), F32)],
        scratch_shapes=[pltpu.VMEM((tm, D_MODEL), BF16)],
        compiler_params=_params(2),
        name="inproj",
    )(x, x, x, g, wm, ws, nrm, cw, cb)


def _mixer(um, sm, mp):
    bsz, seq, _ = um.shape
    nch = seq // CHUNK

    def group(name):
        base = GROUPS.index(name) * N_HEADS
        return pl.BlockSpec((1, seq, DV), lambda b, h: (b, 0, base + h))

    def head(arr):
        nd = arr.ndim
        return pl.BlockSpec((1,) + arr.shape[1:], lambda b, h: (h,) + (0,) * (nd - 1))

    plist = [mp[n] for n in ("w2", "b2", "selrow", "bifrow")]
    in_specs = [group(n) for n in GROUPS] + [pl.BlockSpec((1, seq, LANES), lambda b, h: (b, 0, 0))] \
        + [head(p) for p in plist]
    tok_bf = pltpu.VMEM((2, seq, DK), BF16)
    chunk_t_bf = pltpu.VMEM((2, nch, DK, CHUNK), BF16)
    chunk_tile = pltpu.VMEM((2, nch, SUBLANES, LANES), F32)
    slot_bf = pltpu.VMEM((PIPE_SLOTS, 2, CHUNK, CHUNK), BF16)
    slot_f32 = pltpu.VMEM((PIPE_SLOTS, 2, CHUNK, LANES), F32)
    slot_tile = pltpu.VMEM((PIPE_SLOTS, 2, SUBLANES, LANES), F32)
    return pl.pallas_call(
        _mixer_kernel,
        grid=(bsz, N_HEADS),
        in_specs=in_specs,
        out_specs=pl.BlockSpec((1, seq, DV), lambda b, h: (b, 0, h)),
        out_shape=jax.ShapeDtypeStruct((bsz, seq, N_HEADS * DV), BF16),
        scratch_shapes=[slot_f32,
                        tok_bf, tok_bf, chunk_t_bf,
                        pltpu.VMEM((2, nch, DK, LANES), F32),
                        pltpu.VMEM((nch, SUBLANES, CHUNK), F32),
                        pltpu.VMEM((nch, SUBLANES, CHUNK), F32),
                        pltpu.VMEM((2, seq, LANES), F32),
                        pltpu.VMEM((seq, CHUNK), F32),
                        pltpu.VMEM((nch, DK, CHUNK), BF16),
                        chunk_tile, chunk_tile, chunk_tile, chunk_tile,
                        slot_bf, slot_bf, slot_bf,
                        slot_f32, slot_f32,
                        slot_tile, slot_tile,
                        pltpu.VMEM((2, seq, DV), F32), pltpu.VMEM((2, seq, DV), F32),
                        pltpu.VMEM((2, DK, DV), F32), pltpu.VMEM((2, DK, DV), F32),
                        pltpu.VMEM((2, SUBLANES, DK), F32)],
        compiler_params=_params(2),
        name="mixer",
    )(*([um] * len(GROUPS)), sm, *plist)


def _prep_layer(w_in, gla_w2_fwd, gla_b2_fwd, gla_w2_bwd, gla_b2_bwd, gla_norm, ml_conv_w, ml_conv_b, ml_b_if,
                ml_norm):
    sizes = (512, 512, 1024, 1024, 32, 1024, 1024, 1024, 16, 1024, 1024)
    offs = [0]
    for s in sizes:
        offs.append(offs[-1] + s)
    gq, gk, gv, gr, glr, mqk, mv, mo, mif, gate_a, gate_b = [w_in[:, offs[i]:offs[i + 1]] for i in range(len(sizes))]

    def pair(a, b):
        hd = lambda w: w.reshape(D_MODEL, N_HEADS, DK)
        return jnp.concatenate([hd(a), hd(b)], axis=2).reshape(D_MODEL, N_HEADS * 2 * DK)

    parts = dict(gqk=pair(gq, gk), gv=gv, gr=gr, mqk=pair(mqk[:, :512], mqk[:, 512:]), mv=mv, mo=mo, ga=gate_a,
                 gb=gate_b)
    wm = jnp.concatenate([parts[n] for n in GROUPS[:PLAIN_GROUPS] + W_GATES], axis=1).astype(BF16)
    nrm = jnp.stack([gla_norm, ml_norm])
    ws = jnp.concatenate([glr, mif, jnp.zeros((D_MODEL, LANES - 48), F32)], axis=1).astype(BF16)

    hd = lambda a: a.reshape(a.shape[:-1] + (N_HEADS, DK))

    def w2pad(w2, row0):
        w = jnp.moveaxis(hd(w2), 1, 0)
        return jnp.pad(w, ((0, 0), (row0, LANES - row0 - GLA_LOWRANK), (0, 0)))

    w2 = jnp.concatenate([w2pad(gla_w2_fwd, SM_LR_F), w2pad(gla_w2_bwd, SM_LR_B)], axis=2).astype(BF16)
    b2 = jnp.concatenate([hd(gla_b2_fwd), hd(gla_b2_bwd)], axis=1)[:, None, :]

    g_idx = jnp.arange(4)
    h_idx = jnp.arange(N_HEADS)
    src = SM_IF + 4 * g_idx[None, :] + h_idx[:, None]
    selrow = jnp.zeros((N_HEADS, SUBLANES, LANES), F32).at[h_idx[:, None], g_idx[None, :], src].set(1.0)
    bif = jnp.pad(ml_b_if.reshape(4, N_HEADS).T, ((0, 0), (0, SUBLANES - 4)))
    bifrow = jnp.broadcast_to(bif[:, :, None], (N_HEADS, SUBLANES, GATE_ROWS))

    cw = jnp.concatenate([hd(ml_conv_w[:, :512]), hd(ml_conv_w[:, 512:])], axis=2).reshape(3, GROUP_W)
    cb = jnp.concatenate([hd(ml_conv_b[:512]), hd(ml_conv_b[512:])], axis=1).reshape(1, GROUP_W)
    mp = dict(w2=w2, b2=b2, selrow=selrow.astype(BF16), bifrow=bifrow)
    return wm, ws, nrm, cw, cb, mp


def kernel(x_prompt, x_sample, g_ffn1, w_ffn1_in, w_ffn1_out, g_mix, w_in, gla_w2_fwd, gla_b2_fwd, gla_w2_bwd,
           gla_b2_bwd, gla_norm, ml_conv_w, ml_conv_b, ml_b_if, ml_norm, w_out, g_ffn2, w_ffn2_in, w_ffn2_out,
           g_final):
    depth = w_in.shape[0]
    layers = []
    for l in range(depth):
        layers.append(dict(
            g1=g_ffn1[l][None, :], w1i=w_ffn1_in[l].astype(BF16), w1o=w_ffn1_out[l].astype(BF16),
            gm=g_mix[l][None, :],
            mix=_prep_layer(w_in[l], gla_w2_fwd[l], gla_b2_fwd[l], gla_w2_bwd[l], gla_b2_bwd[l], gla_norm[l],
                            ml_conv_w[l], ml_conv_b[l], ml_b_if[l], ml_norm[l]),
            wo=w_out[l].astype(BF16), g2=g_ffn2[l][None, :], w2i=w_ffn2_in[l].astype(BF16),
            w2o=w_ffn2_out[l].astype(BF16)))
    gf = g_final[None, :]

    def trunk(x):
        bsz, seq, _ = x.shape
        for l, p in enumerate(layers):
            wm, ws, nrm, cw, cb, mp = p["mix"]
            x1 = _ffn1(x.reshape(bsz * seq, D_MODEL), p["g1"], p["w1i"], p["w1o"])
            um, sm = _inproj(x1.reshape(bsz, seq, D_MODEL), p["gm"], wm, ws, nrm, cw, cb)
            merged = _mixer(um, sm, mp)
            x = _outffn(x1, merged.reshape(bsz * seq, D_MODEL), p["wo"], p["g2"], p["w2i"], p["w2o"], gf,
                        final_norm=(l == depth - 1))
            x = x.reshape(bsz, seq, D_MODEL)
        return x

    return trunk(x_prompt), trunk(x_sample)
```

```python
import functools

import jax
import jax.numpy as jnp
from jax import lax
from jax.experimental import pallas as pl
from jax.experimental.pallas import tpu as pltpu

F32 = jnp.float32
BF16 = jnp.bfloat16

D_MODEL = 1024
D_FF = 2816
N_HEADS = 4
DK = 128
DV = 256
GLA_LOWRANK = 16
GLA_TAU = 16.0
EPS = 1e-6
QK_SCALE = DK ** -0.5
LOG2E = 1.4426950408889634

LANES = 128
SUBLANES = 8
CHUNK = 128
GLA_MID = CHUNK // 2
TM_FFN = 512
TF = 256
TN_IN = 256
PAR_UNROLL = 4
GATE_ROWS = 512
GLA_GROUP = 4
SCAN_GROUP = 2
PIPE_SLOTS = 2 * max(GLA_GROUP, SCAN_GROUP)
VMEM_LIMIT = 56 * 1024 * 1024

GROUP_W = N_HEADS * DV
GROUPS = ("gqk", "gv", "mqk", "mv", "wa", "wb")
PLAIN_GROUPS = 4
W_GATES = ("gr", "ga", "mo", "gb")
N_MAIN = len(GROUPS) * GROUP_W
SM_LR_F, SM_LR_B, SM_IF = 0, GLA_LOWRANK, 2 * GLA_LOWRANK


def _dot(a, b):
    return jnp.dot(a, b, preferred_element_type=F32)


def _dot_nt(a, b):
    return lax.dot_general(a, b, (((1,), (1,)), ((), ())), preferred_element_type=F32)


def _rms(x, g):
    return x * lax.rsqrt(jnp.mean(x * x, axis=-1, keepdims=True) + EPS) * g


def _logsig(z):
    return jnp.minimum(z, 0.0) - jnp.log(1.0 + jnp.exp(-jnp.abs(z)))


def _split_hi_lo(x):
    hi = x.astype(BF16)
    lo = (x - hi.astype(F32)).astype(BF16)
    return hi, lo


def _sel_dot(sel_bf16, x):
    hi, lo = _split_hi_lo(x)
    return _dot(sel_bf16, hi) + _dot(sel_bf16, lo)


def _dot_sel(x, sel_bf16):
    hi, lo = _split_hi_lo(x)
    return _dot(hi, sel_bf16) + _dot(lo, sel_bf16)


def _tile(x):
    return jnp.broadcast_to(x, (SUBLANES, LANES))


def _swiglu_into(h_scr, win_ref, wout_ref, act_scr):
    for j in range(D_FF // TF):
        a = _dot(h_scr[...], win_ref[:, j * TF:(j + 1) * TF])
        g = _dot(h_scr[...], win_ref[:, D_FF + j * TF:D_FF + (j + 1) * TF])
        act_scr[:, j * TF:(j + 1) * TF] = (a * jax.nn.sigmoid(a) * g).astype(BF16)
    return _dot(act_scr[...], wout_ref[...])


def _ffn1_kernel(x_ref, g_ref, win_ref, wout_ref, o_ref, h_scr, act_scr):
    x = x_ref[...]
    h_scr[...] = _rms(x, g_ref[...]).astype(BF16)
    o_ref[...] = x + 0.5 * _swiglu_into(h_scr, win_ref, wout_ref, act_scr)


def _outffn_kernel(x_ref, m_ref, wo_ref, g_ref, win_ref, wout_ref, gf_ref, o_ref, h_scr, act_scr, *, final_norm):
    x = x_ref[...] + _dot(m_ref[...], wo_ref[...])
    h_scr[...] = _rms(x, g_ref[...]).astype(BF16)
    x = x + 0.5 * _swiglu_into(h_scr, win_ref, wout_ref, act_scr)
    o_ref[...] = _rms(x, gf_ref[...]) if final_norm else x


def _inproj_kernel(x_ref, xp_ref, xn_ref, g_ref, wm_ref, ws_ref, nrm_ref, cw_ref, cb_ref, um_ref, sm_ref, h_scr):
    tm = x_ref.shape[1]
    h_scr[...] = _rms(x_ref[0], g_ref[...]).astype(BF16)
    i = pl.program_id(1)
    r16 = lax.broadcasted_iota(jnp.int32, (2 * SUBLANES, 1), 0)
    inside = jnp.where(r16 < SUBLANES, jnp.where(i > 0, 1.0, 0.0), jnp.where(i < pl.num_programs(1) - 1, 1.0, 0.0))
    h_halo = _rms(jnp.concatenate([xp_ref[0], xn_ref[0]], axis=0), g_ref[...]).astype(BF16)
    lane_q = lax.broadcasted_iota(jnp.int32, (1, TN_IN), 1) % (2 * DK) < DK
    q_scale = jnp.where(lane_q, QK_SCALE, 1.0)
    def plain(name, j):
        cols = slice(GROUPS.index(name) * GROUP_W + j * TN_IN, GROUPS.index(name) * GROUP_W + (j + 1) * TN_IN)
        um_ref[0, :, cols] = _dot(h_scr[...], wm_ref[:, cols]).astype(BF16)

    def conv(name, j):
        cols = slice(GROUPS.index(name) * GROUP_W + j * TN_IN, GROUPS.index(name) * GROUP_W + (j + 1) * TN_IN)
        cc = slice(j * TN_IN, (j + 1) * TN_IN)
        u = _dot(h_scr[...], wm_ref[:, cols])
        halo = _dot(h_halo, wm_ref[:, cols]) * inside
        xe = jnp.concatenate([halo[:SUBLANES], u, halo[SUBLANES:]], axis=0)
        prev = pltpu.roll(xe, 1, 0)[SUBLANES:SUBLANES + tm]
        nxt = pltpu.roll(xe, tm + 2 * SUBLANES - 1, 0)[SUBLANES:SUBLANES + tm]
        cv = cb_ref[:, cc] + prev * cw_ref[0:1, cc] + u * cw_ref[1:2, cc] + nxt * cw_ref[2:3, cc]
        um_ref[0, :, cols] = (cv * jax.nn.sigmoid(cv) * q_scale).astype(BF16)

    def gates(br, j):
        out_gate, branch_gate = ((PLAIN_GROUPS + 2 * br + t) * GROUP_W + j * TN_IN for t in (0, 1))
        u = _dot(h_scr[...], wm_ref[:, out_gate:out_gate + TN_IN])
        out_act = u * jax.nn.sigmoid(u) if br == 0 else jax.nn.sigmoid(u)
        w = out_act * jax.nn.sigmoid(_dot(h_scr[...], wm_ref[:, branch_gate:branch_gate + TN_IN]))
        dst = (PLAIN_GROUPS + br) * GROUP_W + j * TN_IN
        um_ref[0, :, dst:dst + TN_IN] = (w * nrm_ref[br:br + 1, j * TN_IN:(j + 1) * TN_IN]).astype(BF16)

    n_j = GROUP_W // TN_IN
    heavy_tasks = [t for j in range(n_j) for t in ((conv, "mqk", j), (gates, 0, j), (gates, 1, j))]
    light_tasks = [(name, j) for j in range(n_j) for name in ("gqk", "gv", "mv")]
    for heavy, light in zip(heavy_tasks, light_tasks):
        heavy[0](*heavy[1:])
        plain(*light)
    sm_ref[0] = _dot(h_scr[...], ws_ref[...])


def _mixer_kernel(gqk_ref, gv_ref, mqk_ref, mv_ref, wa_ref, wb_ref, sm_ref,
                  w2_ref, b2_ref, selrow_ref, bifrow_ref,
                  out_ref,
                  bcum_scr, att_scr, qin_scr, kst_t_scr, dcol_scr, gt_scr, grow_scr, bt_scr, qkt_scr, kt_scr,
                  a_scr, g_scr, mprev_scr, mnew_scr,
                  sml_scr, qw_scr, ks_t_scr, floor_scr, dint_scr, dec_scr, dn_scr, og_scr, hm_scr, s_scr, c_scr, n_scr):
    seq = gqk_ref.shape[1]
    nch = seq // CHUNK
    ri = lax.broadcasted_iota(jnp.int32, (CHUNK, CHUNK), 0)
    ci = lax.broadcasted_iota(jnp.int32, (CHUNK, CHUNK), 1)
    lower = ci <= ri
    upper = ci >= ri
    eye = ci == ri
    lower_bf = lower.astype(BF16)
    upper_bf = upper.astype(BF16)
    dirs = ((lower, GLA_MID - 1, CHUNK - 1), (upper, GLA_MID, 0))

    def chunk_rows(c):
        return pl.ds(pl.multiple_of(c * CHUNK, CHUNK), CHUNK)

    def to_col(row):
        return jnp.sum(jnp.where(eye, row, 0.0), axis=1, keepdims=True)

    def act_body(j, carry):
        rows = pl.ds(pl.multiple_of(j * GATE_ROWS, GATE_ROWS), GATE_ROWS)
        sm = sm_ref[0, rows, :]
        hm_scr[0, rows, :] = _logsig(_dot(sm.astype(BF16), w2_ref[0]) + b2_ref[0]) * (LOG2E / GLA_TAU)
        hi, lo = _split_hi_lo(sm)
        g_t = _dot_nt(selrow_ref[0], hi) + _dot_nt(selrow_ref[0], lo) + bifrow_ref[0]
        is_forget = lax.broadcasted_iota(jnp.int32, (SUBLANES, GATE_ROWS), 0) % 2 == 1
        g_t = jnp.where(is_forget, _logsig(g_t), g_t) * LOG2E
        for jj in range(GATE_ROWS // CHUNK):
            gt_scr[j * (GATE_ROWS // CHUNK) + jj] = g_t[:, jj * CHUNK:(jj + 1) * CHUNK]
        return carry

    lax.fori_loop(0, seq // GATE_ROWS, act_body, 0)

    row8 = lax.broadcasted_iota(jnp.int32, (SUBLANES, CHUNK), 0)

    def gate_chunk(c, slot):
        rows = chunk_rows(c)
        bcum_scr[slot, 0] = _sel_dot(lower_bf, hm_scr[0, rows, :DK])
        bcum_scr[slot, 1] = _sel_dot(upper_bf, hm_scr[0, rows, DK:])
        r = gt_scr[c]
        cum_f = _dot_sel(r, upper_bf)
        cum_b = _dot_sel(r, lower_bf)
        i_f, b_f, i_b, b_b = r[0:1], cum_f[1:2], r[2:3], cum_b[3:4]
        c_f = i_f - b_f
        c_b = i_b - b_b
        g8 = jnp.zeros((SUBLANES, CHUNK), F32)
        for idx, val in enumerate((i_f, b_f, i_b, b_b, c_f, c_b)):
            g8 = jnp.where(row8 == idx, val, g8)
        grow_scr[c] = g8
        for d, b, cc in ((0, b_f, c_f), (1, b_b, c_b)):
            l_idx = dirs[d][2]
            a = b[:, l_idx:l_idx + 1]
            a_scr[d, c] = _tile(a)
            g_scr[d, c] = _tile(a + jnp.max(cc, axis=1, keepdims=True))
            bt_scr[d, rows, :] = jnp.broadcast_to(to_col(b), (CHUNK, LANES))

    def gla_chunk(c, slot):
        rows = chunk_rows(c)
        qk = gqk_ref[0, rows, :].astype(F32)
        q = qk[:, :DK] * QK_SCALE
        k = qk[:, DK:]
        for d in (0, 1):
            mask, r_idx, l_idx = dirs[d]
            b = bcum_scr[slot, d]
            rho = b[r_idx:r_idx + 1, :]
            b_last = b[l_idx:l_idx + 1, :]
            q_mid = (q * jnp.exp2(b - rho)).astype(BF16)
            k_mid = (k * jnp.exp2(rho - b)).astype(BF16)
            att_scr[d, rows, :] = jnp.where(mask, _dot_nt(q_mid, k_mid), 0.0).astype(BF16)
            qin_scr[d, rows, :] = (q * jnp.exp2(b)).astype(BF16)
            kst_t_scr[d, c] = (k * jnp.exp2(b_last - b)).astype(BF16).T
            dcol_scr[d, c] = jnp.broadcast_to(to_col(jnp.exp2(b_last)), (DK, LANES))
        mqk = mqk_ref[0, rows, :]
        qkt_scr[rows, :] = _dot_nt(mqk[:, :DK], mqk[:, DK:])
        kt_scr[c] = mqk[:, DK:].T

    def pipelined(produce, consume, group):
        for j in range(group):
            produce(j, j)

        def body(q, carry):
            for half in (0, 1):
                base = (2 * q + half) * group
                for j in range(group):
                    produce(jnp.minimum(base + group + j, nch - 1), (1 - half) * group + j)
                for j in range(group):
                    consume(base + j, half * group + j)
            return carry

        lax.fori_loop(0, nch // (2 * group), body, 0)

    pipelined(gate_chunk, gla_chunk, GLA_GROUP)

    def m_body(i, carry):
        m_f, m_b = carry
        cb = nch - 1 - i
        mprev_scr[0, i] = m_f
        mprev_scr[1, cb] = m_b
        m_f = jnp.maximum(a_scr[0, i] + m_f, g_scr[0, i])
        m_b = jnp.maximum(a_scr[1, cb] + m_b, g_scr[1, cb])
        mnew_scr[0, i] = m_f
        mnew_scr[1, cb] = m_b
        return m_f, m_b

    m0 = jnp.full((SUBLANES, LANES), -jnp.inf, F32)
    lax.fori_loop(0, nch, m_body, (m0, m0))

    def prep_dir(d, c, slot):
        rows = chunk_rows(c)
        mask = dirs[d][0]
        g8 = grow_scr[c]
        b_t = bt_scr[d, rows, :]
        c_s = g8[4 + d:5 + d, :]
        m_prev = mprev_scr[d, c][0:1, 0:1]
        m_new = mnew_scr[d, c][0:1, 0:1]
        b_last = a_scr[d, c][0:1, 0:1]
        log_d = jnp.where(mask, b_t + c_s, -jnp.inf)
        m_inter = b_t + m_prev
        m_t = jnp.maximum(m_inter, jnp.max(log_d, axis=1, keepdims=True))
        sml = qkt_scr[rows, :] * jnp.exp2(log_d - m_t)
        sml_scr[slot, d] = sml.astype(BF16)
        dint_scr[slot, d] = jnp.broadcast_to(jnp.sum(sml, axis=1, keepdims=True), (CHUNK, LANES))
        qw_scr[slot, d] = (jnp.exp2(m_inter - m_t) * mqk_ref[0, rows, :DK].astype(F32)).astype(BF16)
        floor_scr[slot, d] = jnp.exp2(-m_t)
        w_st = jnp.broadcast_to(jnp.exp2(b_last + c_s - m_new), (SUBLANES, CHUNK))
        ks_t_scr[slot, d] = (kt_scr[c].astype(F32) * w_st[0:1, :]).astype(BF16)
        dn_scr[slot, d] = _dot(w_st.astype(BF16), mqk_ref[0, rows, DK:])
        dec_scr[slot, d] = _tile(jnp.exp2(b_last + m_prev - m_new))

    def scan_dir(d, c, slot):
        rows = chunk_rows(c)
        v = gv_ref[0, rows, :]
        s = s_scr[d]
        og_scr[d, rows, :] = _dot(jnp.concatenate([att_scr[d, rows, :], qin_scr[d, rows, :]], axis=1),
                                  jnp.concatenate([v, s.astype(BF16)], axis=0))
        dcol = dcol_scr[d, c]
        s_scr[d] = s * jnp.concatenate([dcol, dcol], axis=1) + _dot(kst_t_scr[d, c], v)
        mv = mv_ref[0, rows, :]
        cst = c_scr[d]
        nst = n_scr[d]
        qw = qw_scr[slot, d]
        num = _dot(jnp.concatenate([sml_scr[slot, d], qw], axis=1), jnp.concatenate([mv, cst.astype(BF16)], axis=0))
        den = dint_scr[slot, d] + jnp.sum(qw.astype(F32) * nst[0:1, :], axis=1, keepdims=True)
        rden = 1.0 / jnp.maximum(jnp.abs(den), floor_scr[slot, d])
        hm_scr[d, rows, :] = num * jnp.concatenate([rden, rden], axis=1)
        decay = dec_scr[slot, d][0:1, 0:1]
        c_scr[d] = decay * cst + _dot(ks_t_scr[slot, d], mv)
        n_scr[d] = decay * nst + dn_scr[slot, d]

    s_scr[...] = jnp.zeros_like(s_scr)
    c_scr[...] = jnp.zeros_like(c_scr)
    n_scr[...] = jnp.zeros_like(n_scr)

    def prep_step(i, slot):
        prep_dir(0, i, slot)
        prep_dir(1, nch - 1 - i, slot)

    def scan_step(i, slot):
        scan_dir(0, i, slot)
        scan_dir(1, nch - 1 - i, slot)

    pipelined(prep_step, scan_step, SCAN_GROUP)

    def merge_body(c, carry):
        rows = chunk_rows(c)
        og = og_scr[0, rows, :] + og_scr[1, rows, :]
        hm = hm_scr[0, rows, :] + hm_scr[1, rows, :]
        o_a = og * lax.rsqrt(jnp.mean(og * og, axis=-1, keepdims=True) + EPS)
        o_b = hm * lax.rsqrt(jnp.mean(hm * hm, axis=-1, keepdims=True) + EPS)
        merged = o_a * wa_ref[0, rows, :].astype(F32) + o_b * wb_ref[0, rows, :].astype(F32)
        out_ref[0, rows, :] = merged.astype(BF16)
        return carry

    lax.fori_loop(0, nch, merge_body, 0, unroll=PAR_UNROLL)


def _resident(shape):
    nd = len(shape)
    return pl.BlockSpec(shape, lambda *_: (0,) * nd, pipeline_mode=pl.Buffered(1))


def _params(n_axes):
    return pltpu.CompilerParams(dimension_semantics=("arbitrary",) * n_axes, vmem_limit_bytes=VMEM_LIMIT)


def _ffn1(x2d, g, win, wout):
    m = x2d.shape[0]
    tile = pl.BlockSpec((TM_FFN, D_MODEL), lambda i: (i, 0))
    return pl.pallas_call(
        _ffn1_kernel,
        grid=(m // TM_FFN,),
        in_specs=[tile, _resident(g.shape), _resident(win.shape), _resident(wout.shape)],
        out_specs=tile,
        out_shape=jax.ShapeDtypeStruct((m, D_MODEL), F32),
        scratch_shapes=[pltpu.VMEM((TM_FFN, D_MODEL), BF16), pltpu.VMEM((TM_FFN, D_FF), BF16)],
        compiler_params=_params(1),
        name="ffn1",
    )(x2d, g, win, wout)


def _outffn(x2d, merged2d, wo, g, win, wout, gf, final_norm):
    m = x2d.shape[0]
    tile = pl.BlockSpec((TM_FFN, D_MODEL), lambda i: (i, 0))
    return pl.pallas_call(
        functools.partial(_outffn_kernel, final_norm=final_norm),
        grid=(m // TM_FFN,),
        in_specs=[tile, tile, _resident(wo.shape), _resident(g.shape), _resident(win.shape), _resident(wout.shape),
                  _resident(gf.shape)],
        out_specs=tile,
        out_shape=jax.ShapeDtypeStruct((m, D_MODEL), F32),
        scratch_shapes=[pltpu.VMEM((TM_FFN, D_MODEL), BF16), pltpu.VMEM((TM_FFN, D_FF), BF16)],
        compiler_params=_params(1),
        name="outffn",
    )(x2d, merged2d, wo, g, win, wout, gf)


def _inproj(x, g, wm, ws, nrm, cw, cb):
    bsz, seq, _ = x.shape
    tm = TM_FFN
    per_tile = tm // SUBLANES
    last = seq // SUBLANES - 1
    return pl.pallas_call(
        _inproj_kernel,
        grid=(bsz, seq // tm),
        in_specs=[pl.BlockSpec((1, tm, D_MODEL), lambda b, i: (b, i, 0)),
                  pl.BlockSpec((1, SUBLANES, D_MODEL), lambda b, i: (b, jnp.maximum(i * per_tile - 1, 0), 0)),
                  pl.BlockSpec((1, SUBLANES, D_MODEL), lambda b, i: (b, jnp.minimum((i + 1) * per_tile, last), 0)),
                  _resident(g.shape), _resident(wm.shape), _resident(ws.shape), _resident(nrm.shape),
                  _resident(cw.shape), _resident(cb.shape)],
        out_specs=[pl.BlockSpec((1, tm, N_MAIN), lambda b, i: (b, i, 0)),
                   pl.BlockSpec((1, tm, LANES), lambda b, i: (b, i, 0))],
        out_shape=[jax.ShapeDtypeStruct((bsz, seq, N_MAIN), BF16),
                   jax.ShapeDtypeStruct((bsz, seq, LANES), F32)],
        scratch_shapes=[pltpu.VMEM((tm, D_MODEL), BF16)],
        compiler_params=_params(2),
        name="inproj",
    )(x, x, x, g, wm, ws, nrm, cw, cb)


def _mixer(um, sm, mp):
    bsz, seq, _ = um.shape
    nch = seq // CHUNK

    def group(name):
        base = GROUPS.index(name) * N_HEADS
        return pl.BlockSpec((1, seq, DV), lambda b, h: (b, 0, base + h))

    def head(arr):
        nd = arr.ndim
        return pl.BlockSpec((1,) + arr.shape[1:], lambda b, h: (h,) + (0,) * (nd - 1))

    plist = [mp[n] for n in ("w2", "b2", "selrow", "bifrow")]
    in_specs = [group(n) for n in GROUPS] + [pl.BlockSpec((1, seq, LANES), lambda b, h: (b, 0, 0))] \
        + [head(p) for p in plist]
    tok_bf = pltpu.VMEM((2, seq, DK), BF16)
    chunk_t_bf = pltpu.VMEM((2, nch, DK, CHUNK), BF16)
    chunk_tile = pltpu.VMEM((2, nch, SUBLANES, LANES), F32)
    slot_bf = pltpu.VMEM((PIPE_SLOTS, 2, CHUNK, CHUNK), BF16)
    slot_f32 = pltpu.VMEM((PIPE_SLOTS, 2, CHUNK, LANES), F32)
    slot_tile = pltpu.VMEM((PIPE_SLOTS, 2, SUBLANES, LANES), F32)
    return pl.pallas_call(
        _mixer_kernel,
        grid=(bsz, N_HEADS),
        in_specs=in_specs,
        out_specs=pl.BlockSpec((1, seq, DV), lambda b, h: (b, 0, h)),
        out_shape=jax.ShapeDtypeStruct((bsz, seq, N_HEADS * DV), BF16),
        scratch_shapes=[slot_f32,
                        tok_bf, tok_bf, chunk_t_bf,
                        pltpu.VMEM((2, nch, DK, LANES), F32),
                        pltpu.VMEM((nch, SUBLANES, CHUNK), F32),
                        pltpu.VMEM((nch, SUBLANES, CHUNK), F32),
                        pltpu.VMEM((2, seq, LANES), F32),
                        pltpu.VMEM((seq, CHUNK), F32),
                        pltpu.VMEM((nch, DK, CHUNK), BF16),
                        chunk_tile, chunk_tile, chunk_tile, chunk_tile,
                        slot_bf, slot_bf, slot_bf,
                        slot_f32, slot_f32,
                        slot_tile, slot_tile,
                        pltpu.VMEM((2, seq, DV), F32), pltpu.VMEM((2, seq, DV), F32),
                        pltpu.VMEM((2, DK, DV), F32), pltpu.VMEM((2, DK, DV), F32),
                        pltpu.VMEM((2, SUBLANES, DK), F32)],
        compiler_params=_params(2),
        name="mixer",
    )(*([um] * len(GROUPS)), sm, *plist)


def _prep_layer(w_in, gla_w2_fwd, gla_b2_fwd, gla_w2_bwd, gla_b2_bwd, gla_norm, ml_conv_w, ml_conv_b, ml_b_if,
                ml_norm):
    sizes = (512, 512, 1024, 1024, 32, 1024, 1024, 1024, 16, 1024, 1024)
    offs = [0]
    for s in sizes:
        offs.append(offs[-1] + s)
    gq, gk, gv, gr, glr, mqk, mv, mo, mif, gate_a, gate_b = [w_in[:, offs[i]:offs[i + 1]] for i in range(len(sizes))]

    def pair(a, b):
        hd = lambda w: w.reshape(D_MODEL, N_HEADS, DK)
        return jnp.concatenate([hd(a), hd(b)], axis=2).reshape(D_MODEL, N_HEADS * 2 * DK)

    parts = dict(gqk=pair(gq, gk), gv=gv, gr=gr, mqk=pair(mqk[:, :512], mqk[:, 512:]), mv=mv, mo=mo, ga=gate_a,
                 gb=gate_b)
    wm = jnp.concatenate([parts[n] for n in GROUPS[:PLAIN_GROUPS] + W_GATES], axis=1).astype(BF16)
    nrm = jnp.stack([gla_norm, ml_norm])
    ws = jnp.concatenate([glr, mif, jnp.zeros((D_MODEL, LANES - 48), F32)], axis=1).astype(BF16)

    hd = lambda a: a.reshape(a.shape[:-1] + (N_HEADS, DK))

    def w2pad(w2, row0):
        w = jnp.moveaxis(hd(w2), 1, 0)
        return jnp.pad(w, ((0, 0), (row0, LANES - row0 - GLA_LOWRANK), (0, 0)))

    w2 = jnp.concatenate([w2pad(gla_w2_fwd, SM_LR_F), w2pad(gla_w2_bwd, SM_LR_B)], axis=2).astype(BF16)
    b2 = jnp.concatenate([hd(gla_b2_fwd), hd(gla_b2_bwd)], axis=1)[:, None, :]

    g_idx = jnp.arange(4)
    h_idx = jnp.arange(N_HEADS)
    src = SM_IF + 4 * g_idx[None, :] + h_idx[:, None]
    selrow = jnp.zeros((N_HEADS, SUBLANES, LANES), F32).at[h_idx[:, None], g_idx[None, :], src].set(1.0)
    bif = jnp.pad(ml_b_if.reshape(4, N_HEADS).T, ((0, 0), (0, SUBLANES - 4)))
    bifrow = jnp.broadcast_to(bif[:, :, None], (N_HEADS, SUBLANES, GATE_ROWS))

    cw = jnp.concatenate([hd(ml_conv_w[:, :512]), hd(ml_conv_w[:, 512:])], axis=2).reshape(3, GROUP_W)
    cb = jnp.concatenate([hd(ml_conv_b[:512]), hd(ml_conv_b[512:])], axis=1).reshape(1, GROUP_W)
    mp = dict(w2=w2, b2=b2, selrow=selrow.astype(BF16), bifrow=bifrow)
    return wm, ws, nrm, cw, cb, mp


def kernel(x_prompt, x_sample, g_ffn1, w_ffn1_in, w_ffn1_out, g_mix, w_in, gla_w2_fwd, gla_b2_fwd, gla_w2_bwd,
           gla_b2_bwd, gla_norm, ml_conv_w, ml_conv_b, ml_b_if, ml_norm, w_out, g_ffn2, w_ffn2_in, w_ffn2_out,
           g_final):
    depth = w_in.shape[0]
    layers = []
    for l in range(depth):
        layers.append(dict(
            g1=g_ffn1[l][None, :], w1i=w_ffn1_in[l].astype(BF16), w1o=w_ffn1_out[l].astype(BF16),
            gm=g_mix[l][None, :],
            mix=_prep_layer(w_in[l], gla_w2_fwd[l], gla_b2_fwd[l], gla_w2_bwd[l], gla_b2_bwd[l], gla_norm[l],
                            ml_conv_w[l], ml_conv_b[l], ml_b_if[l], ml_norm[l]),
            wo=w_out[l].astype(BF16), g2=g_ffn2[l][None, :], w2i=w_ffn2_in[l].astype(BF16),
            w2o=w_ffn2_out[l].astype(BF16)))
    gf = g_final[None, :]

    def trunk(x):
        bsz, seq, _ = x.shape
        for l, p in enumerate(layers):
            wm, ws, nrm, cw, cb, mp = p["mix"]
            x1 = _ffn1(x.reshape(bsz * seq, D_MODEL), p["g1"], p["w1i"], p["w1o"])
            um, sm = _inproj(x1.reshape(bsz, seq, D_MODEL), p["gm"], wm, ws, nrm, cw, cb)
            merged = _mixer(um, sm, mp)
            x = _outffn(x1, merged.reshape(bsz * seq, D_MODEL), p["wo"], p["g2"], p["w2i"], p["w2o"], gf,
                        final_norm=(l == depth - 1))
            x = x.reshape(bsz, seq, D_MODEL)
        return x

    return trunk(x_prompt), trunk(x_sample)
```

```python
import functools

import jax
import jax.numpy as jnp
from jax import lax
from jax.experimental import pallas as pl
from jax.experimental.pallas import tpu as pltpu

F32 = jnp.float32
BF16 = jnp.bfloat16

D_MODEL = 1024
D_FF = 2816
N_HEADS = 4
DK = 128
DV = 256
GLA_LOWRANK = 16
GLA_TAU = 16.0
EPS = 1e-6
QK_SCALE = DK ** -0.5
LOG2E = 1.4426950408889634

LANES = 128
SUBLANES = 8
CHUNK = 128
GLA_MID = CHUNK // 2
TM_FFN = 512
TF = 256
TN_IN = 256
PAR_UNROLL = 4
GATE_ROWS = 512
GLA_GROUP = 4
SCAN_GROUP = 2
PIPE_SLOTS = 2 * max(GLA_GROUP, SCAN_GROUP)
VMEM_LIMIT = 56 * 1024 * 1024

GROUP_W = N_HEADS * DV
GROUPS = ("gqk", "gv", "mqk", "mv", "wa", "wb")
PLAIN_GROUPS = 4
W_GATES = ("gr", "ga", "mo", "gb")
N_MAIN = len(GROUPS) * GROUP_W
SM_LR_F, SM_LR_B, SM_IF = 0, GLA_LOWRANK, 2 * GLA_LOWRANK


def _dot(a, b):
    return jnp.dot(a, b, preferred_element_type=F32)


def _dot_nt(a, b):
    return lax.dot_general(a, b, (((1,), (1,)), ((), ())), preferred_element_type=F32)


def _rms(x, g):
    return x * lax.rsqrt(jnp.mean(x * x, axis=-1, keepdims=True) + EPS) * g


def _logsig(z):
    return jnp.minimum(z, 0.0) - jnp.log(1.0 + jnp.exp(-jnp.abs(z)))


def _split_hi_lo(x):
    hi = x.astype(BF16)
    lo = (x - hi.astype(F32)).astype(BF16)
    return hi, lo


def _sel_dot(sel_bf16, x):
    hi, lo = _split_hi_lo(x)
    return _dot(sel_bf16, hi) + _dot(sel_bf16, lo)


def _dot_sel(x, sel_bf16):
    hi, lo = _split_hi_lo(x)
    return _dot(hi, sel_bf16) + _dot(lo, sel_bf16)


def _tile(x):
    return jnp.broadcast_to(x, (SUBLANES, LANES))


def _swiglu_into(h_scr, win_ref, wout_ref, act_scr):
    for j in range(D_FF // TF):
        a = _dot(h_scr[...], win_ref[:, j * TF:(j + 1) * TF])
        g = _dot(h_scr[...], win_ref[:, D_FF + j * TF:D_FF + (j + 1) * TF])
        act_scr[:, j * TF:(j + 1) * TF] = (a * jax.nn.sigmoid(a) * g).astype(BF16)
    return _dot(act_scr[...], wout_ref[...])


def _ffn1_kernel(x_ref, g_ref, win_ref, wout_ref, o_ref, h_scr, act_scr):
    x = x_ref[...]
    h_scr[...] = _rms(x, g_ref[...]).astype(BF16)
    o_ref[...] = x + 0.5 * _swiglu_into(h_scr, win_ref, wout_ref, act_scr)


def _outffn_kernel(x_ref, m_ref, wo_ref, g_ref, win_ref, wout_ref, gf_ref, o_ref, h_scr, act_scr, *, final_norm):
    x = x_ref[...]
    for hd in range(N_HEADS):
        x = x + _dot(m_ref[0, hd], wo_ref[hd * DV:(hd + 1) * DV, :])
    h_scr[...] = _rms(x, g_ref[...]).astype(BF16)
    x = x + 0.5 * _swiglu_into(h_scr, win_ref, wout_ref, act_scr)
    o_ref[...] = _rms(x, gf_ref[...]) if final_norm else x


def _inproj_kernel(x_ref, xp_ref, xn_ref, g_ref, wm_ref, ws_ref, nrm_ref, cw_ref, cb_ref, um_ref, sm_ref, h_scr):
    tm = x_ref.shape[1]
    h_scr[...] = _rms(x_ref[0], g_ref[...]).astype(BF16)
    i = pl.program_id(1)
    r16 = lax.broadcasted_iota(jnp.int32, (2 * SUBLANES, 1), 0)
    inside = jnp.where(r16 < SUBLANES, jnp.where(i > 0, 1.0, 0.0), jnp.where(i < pl.num_programs(1) - 1, 1.0, 0.0))
    h_halo = _rms(jnp.concatenate([xp_ref[0], xn_ref[0]], axis=0), g_ref[...]).astype(BF16)
    lane_q = lax.broadcasted_iota(jnp.int32, (1, TN_IN), 1) % (2 * DK) < DK
    q_scale = jnp.where(lane_q, QK_SCALE, 1.0)
    def store(col0, val):
        for k in range(TN_IN // DV):
            um_ref[0, col0 // DV + k] = val[:, k * DV:(k + 1) * DV].astype(BF16)

    def plain(name, j):
        cols = slice(GROUPS.index(name) * GROUP_W + j * TN_IN, GROUPS.index(name) * GROUP_W + (j + 1) * TN_IN)
        store(cols.start, _dot(h_scr[...], wm_ref[:, cols]))

    def conv(name, j):
        cols = slice(GROUPS.index(name) * GROUP_W + j * TN_IN, GROUPS.index(name) * GROUP_W + (j + 1) * TN_IN)
        cc = slice(j * TN_IN, (j + 1) * TN_IN)
        u = _dot(h_scr[...], wm_ref[:, cols])
        halo = _dot(h_halo, wm_ref[:, cols]) * inside
        xe = jnp.concatenate([halo[:SUBLANES], u, halo[SUBLANES:]], axis=0)
        prev = pltpu.roll(xe, 1, 0)[SUBLANES:SUBLANES + tm]
        nxt = pltpu.roll(xe, tm + 2 * SUBLANES - 1, 0)[SUBLANES:SUBLANES + tm]
        cv = cb_ref[:, cc] + prev * cw_ref[0:1, cc] + u * cw_ref[1:2, cc] + nxt * cw_ref[2:3, cc]
        store(cols.start, cv * jax.nn.sigmoid(cv) * q_scale)

    def gates(br, j):
        out_gate, branch_gate = ((PLAIN_GROUPS + 2 * br + t) * GROUP_W + j * TN_IN for t in (0, 1))
        u = _dot(h_scr[...], wm_ref[:, out_gate:out_gate + TN_IN])
        out_act = u * jax.nn.sigmoid(u) if br == 0 else jax.nn.sigmoid(u)
        w = out_act * jax.nn.sigmoid(_dot(h_scr[...], wm_ref[:, branch_gate:branch_gate + TN_IN]))
        dst = (PLAIN_GROUPS + br) * GROUP_W + j * TN_IN
        store(dst, w * nrm_ref[br:br + 1, j * TN_IN:(j + 1) * TN_IN])

    n_j = GROUP_W // TN_IN
    heavy_tasks = [t for j in range(n_j) for t in ((conv, "mqk", j), (gates, 0, j), (gates, 1, j))]
    light_tasks = [(name, j) for j in range(n_j) for name in ("gqk", "gv", "mv")]
    for heavy, light in zip(heavy_tasks, light_tasks):
        heavy[0](*heavy[1:])
        plain(*light)
    sm_ref[0] = _dot(h_scr[...], ws_ref[...])


def _mixer_kernel(gqk_ref, gv_ref, mqk_ref, mv_ref, wa_ref, wb_ref, sm_ref,
                  w2_ref, b2_ref, selrow_ref, bifrow_ref,
                  out_ref,
                  bcum_scr, att_scr, qin_scr, kst_t_scr, dcol_scr, gt_scr, grow_scr, bt_scr, qkt_scr, kt_scr,
                  a_scr, g_scr, mprev_scr, mnew_scr,
                  sml_scr, qw_scr, ks_t_scr, floor_scr, dint_scr, dec_scr, dn_scr, og_scr, hm_scr, s_scr, c_scr, n_scr):
    gqk_ref, gv_ref, mqk_ref, mv_ref, wa_ref, wb_ref, out_ref = (
        r.at[0] for r in (gqk_ref, gv_ref, mqk_ref, mv_ref, wa_ref, wb_ref, out_ref))
    seq = gqk_ref.shape[1]
    nch = seq // CHUNK
    ri = lax.broadcasted_iota(jnp.int32, (CHUNK, CHUNK), 0)
    ci = lax.broadcasted_iota(jnp.int32, (CHUNK, CHUNK), 1)
    lower = ci <= ri
    upper = ci >= ri
    eye = ci == ri
    lower_bf = lower.astype(BF16)
    upper_bf = upper.astype(BF16)
    dirs = ((lower, GLA_MID - 1, CHUNK - 1), (upper, GLA_MID, 0))

    def chunk_rows(c):
        return pl.ds(pl.multiple_of(c * CHUNK, CHUNK), CHUNK)

    def to_col(row):
        return jnp.sum(jnp.where(eye, row, 0.0), axis=1, keepdims=True)

    def act_body(j, carry):
        rows = pl.ds(pl.multiple_of(j * GATE_ROWS, GATE_ROWS), GATE_ROWS)
        sm = sm_ref[0, rows, :]
        hm_scr[0, rows, :] = _logsig(_dot(sm.astype(BF16), w2_ref[0]) + b2_ref[0]) * (LOG2E / GLA_TAU)
        hi, lo = _split_hi_lo(sm)
        g_t = _dot_nt(selrow_ref[0], hi) + _dot_nt(selrow_ref[0], lo) + bifrow_ref[0]
        is_forget = lax.broadcasted_iota(jnp.int32, (SUBLANES, GATE_ROWS), 0) % 2 == 1
        g_t = jnp.where(is_forget, _logsig(g_t), g_t) * LOG2E
        for jj in range(GATE_ROWS // CHUNK):
            gt_scr[j * (GATE_ROWS // CHUNK) + jj] = g_t[:, jj * CHUNK:(jj + 1) * CHUNK]
        return carry

    lax.fori_loop(0, seq // GATE_ROWS, act_body, 0)

    row8 = lax.broadcasted_iota(jnp.int32, (SUBLANES, CHUNK), 0)

    def gate_chunk(c, slot):
        rows = chunk_rows(c)
        bcum_scr[slot, 0] = _sel_dot(lower_bf, hm_scr[0, rows, :DK])
        bcum_scr[slot, 1] = _sel_dot(upper_bf, hm_scr[0, rows, DK:])
        r = gt_scr[c]
        cum_f = _dot_sel(r, upper_bf)
        cum_b = _dot_sel(r, lower_bf)
        i_f, b_f, i_b, b_b = r[0:1], cum_f[1:2], r[2:3], cum_b[3:4]
        c_f = i_f - b_f
        c_b = i_b - b_b
        g8 = jnp.zeros((SUBLANES, CHUNK), F32)
        for idx, val in enumerate((i_f, b_f, i_b, b_b, c_f, c_b)):
            g8 = jnp.where(row8 == idx, val, g8)
        grow_scr[c] = g8
        for d, b, cc in ((0, b_f, c_f), (1, b_b, c_b)):
            l_idx = dirs[d][2]
            a = b[:, l_idx:l_idx + 1]
            a_scr[d, c] = _tile(a)
            g_scr[d, c] = _tile(a + jnp.max(cc, axis=1, keepdims=True))
            bt_scr[d, rows, :] = jnp.broadcast_to(to_col(b), (CHUNK, LANES))

    def gla_chunk(c, slot):
        rows = chunk_rows(c)
        qk = gqk_ref[0, rows, :].astype(F32)
        q = qk[:, :DK] * QK_SCALE
        k = qk[:, DK:]
        for d in (0, 1):
            mask, r_idx, l_idx = dirs[d]
            b = bcum_scr[slot, d]
            rho = b[r_idx:r_idx + 1, :]
            b_last = b[l_idx:l_idx + 1, :]
            q_mid = (q * jnp.exp2(b - rho)).astype(BF16)
            k_mid = (k * jnp.exp2(rho - b)).astype(BF16)
            att_scr[d, rows, :] = jnp.where(mask, _dot_nt(q_mid, k_mid), 0.0).astype(BF16)
            qin_scr[d, rows, :] = (q * jnp.exp2(b)).astype(BF16)
            kst_t_scr[d, c] = (k * jnp.exp2(b_last - b)).astype(BF16).T
            dcol_scr[d, c] = jnp.broadcast_to(to_col(jnp.exp2(b_last)), (DK, LANES))
        mqk = mqk_ref[0, rows, :]
        qkt_scr[rows, :] = _dot_nt(mqk[:, :DK], mqk[:, DK:])
        kt_scr[c] = mqk[:, DK:].T

    def pipelined(produce, consume, group):
        for j in range(group):
            produce(j, j)

        def body(q, carry):
            for half in (0, 1):
                base = (2 * q + half) * group
                for j in range(group):
                    produce(jnp.minimum(base + group + j, nch - 1), (1 - half) * group + j)
                for j in range(group):
                    consume(base + j, half * group + j)
            return carry

        lax.fori_loop(0, nch // (2 * group), body, 0)

    pipelined(gate_chunk, gla_chunk, GLA_GROUP)

    def m_body(i, carry):
        m_f, m_b = carry
        cb = nch - 1 - i
        mprev_scr[0, i] = m_f
        mprev_scr[1, cb] = m_b
        m_f = jnp.maximum(a_scr[0, i] + m_f, g_scr[0, i])
        m_b = jnp.maximum(a_scr[1, cb] + m_b, g_scr[1, cb])
        mnew_scr[0, i] = m_f
        mnew_scr[1, cb] = m_b
        return m_f, m_b

    m0 = jnp.full((SUBLANES, LANES), -jnp.inf, F32)
    lax.fori_loop(0, nch, m_body, (m0, m0))

    def prep_dir(d, c, slot):
        rows = chunk_rows(c)
        mask = dirs[d][0]
        g8 = grow_scr[c]
        b_t = bt_scr[d, rows, :]
        c_s = g8[4 + d:5 + d, :]
        m_prev = mprev_scr[d, c][0:1, 0:1]
        m_new = mnew_scr[d, c][0:1, 0:1]
        b_last = a_scr[d, c][0:1, 0:1]
        log_d = jnp.where(mask, b_t + c_s, -jnp.inf)
        m_inter = b_t + m_prev
        m_t = jnp.maximum(m_inter, jnp.max(log_d, axis=1, keepdims=True))
        sml = qkt_scr[rows, :] * jnp.exp2(log_d - m_t)
        sml_scr[slot, d] = sml.astype(BF16)
        dint_scr[slot, d] = jnp.broadcast_to(jnp.sum(sml, axis=1, keepdims=True), (CHUNK, LANES))
        qw_scr[slot, d] = (jnp.exp2(m_inter - m_t) * mqk_ref[0, rows, :DK].astype(F32)).astype(BF16)
        floor_scr[slot, d] = jnp.exp2(-m_t)
        w_st = jnp.broadcast_to(jnp.exp2(b_last + c_s - m_new), (SUBLANES, CHUNK))
        ks_t_scr[slot, d] = (kt_scr[c].astype(F32) * w_st[0:1, :]).astype(BF16)
        dn_scr[slot, d] = _dot(w_st.astype(BF16), mqk_ref[0, rows, DK:])
        dec_scr[slot, d] = _tile(jnp.exp2(b_last + m_prev - m_new))

    def scan_dir(d, c, slot):
        rows = chunk_rows(c)
        v = gv_ref[0, rows, :]
        s = s_scr[d]
        og_scr[d, rows, :] = _dot(jnp.concatenate([att_scr[d, rows, :], qin_scr[d, rows, :]], axis=1),
                                  jnp.concatenate([v, s.astype(BF16)], axis=0))
        dcol = dcol_scr[d, c]
        s_scr[d] = s * jnp.concatenate([dcol, dcol], axis=1) + _dot(kst_t_scr[d, c], v)
        mv = mv_ref[0, rows, :]
        cst = c_scr[d]
        nst = n_scr[d]
        qw = qw_scr[slot, d]
        num = _dot(jnp.concatenate([sml_scr[slot, d], qw], axis=1), jnp.concatenate([mv, cst.astype(BF16)], axis=0))
        den = dint_scr[slot, d] + jnp.sum(qw.astype(F32) * nst[0:1, :], axis=1, keepdims=True)
        rden = 1.0 / jnp.maximum(jnp.abs(den), floor_scr[slot, d])
        hm_scr[d, rows, :] = num * jnp.concatenate([rden, rden], axis=1)
        decay = dec_scr[slot, d][0:1, 0:1]
        c_scr[d] = decay * cst + _dot(ks_t_scr[slot, d], mv)
        n_scr[d] = decay * nst + dn_scr[slot, d]

    s_scr[...] = jnp.zeros_like(s_scr)
    c_scr[...] = jnp.zeros_like(c_scr)
    n_scr[...] = jnp.zeros_like(n_scr)

    def prep_step(i, slot):
        prep_dir(0, i, slot)
        prep_dir(1, nch - 1 - i, slot)

    def scan_step(i, slot):
        scan_dir(0, i, slot)
        scan_dir(1, nch - 1 - i, slot)

    pipelined(prep_step, scan_step, SCAN_GROUP)

    def merge_body(c, carry):
        rows = chunk_rows(c)
        og = og_scr[0, rows, :] + og_scr[1, rows, :]
        hm = hm_scr[0, rows, :] + hm_scr[1, rows, :]
        o_a = og * lax.rsqrt(jnp.mean(og * og, axis=-1, keepdims=True) + EPS)
        o_b = hm * lax.rsqrt(jnp.mean(hm * hm, axis=-1, keepdims=True) + EPS)
        merged = o_a * wa_ref[0, rows, :].astype(F32) + o_b * wb_ref[0, rows, :].astype(F32)
        out_ref[0, rows, :] = merged.astype(BF16)
        return carry

    lax.fori_loop(0, nch, merge_body, 0, unroll=PAR_UNROLL)


def _resident(shape):
    nd = len(shape)
    return pl.BlockSpec(shape, lambda *_: (0,) * nd, pipeline_mode=pl.Buffered(1))


def _params(n_axes):
    return pltpu.CompilerParams(dimension_semantics=("arbitrary",) * n_axes, vmem_limit_bytes=VMEM_LIMIT)


def _ffn1(x2d, g, win, wout):
    m = x2d.shape[0]
    tile = pl.BlockSpec((TM_FFN, D_MODEL), lambda i: (i, 0))
    return pl.pallas_call(
        _ffn1_kernel,
        grid=(m // TM_FFN,),
        in_specs=[tile, _resident(g.shape), _resident(win.shape), _resident(wout.shape)],
        out_specs=tile,
        out_shape=jax.ShapeDtypeStruct((m, D_MODEL), F32),
        scratch_shapes=[pltpu.VMEM((TM_FFN, D_MODEL), BF16), pltpu.VMEM((TM_FFN, D_FF), BF16)],
        compiler_params=_params(1),
        name="ffn1",
    )(x2d, g, win, wout)


def _outffn(x2d, merged, wo, g, win, wout, gf, final_norm):
    m = x2d.shape[0]
    per_seq = merged.shape[2] // TM_FFN
    tile = pl.BlockSpec((TM_FFN, D_MODEL), lambda i: (i, 0))
    heads = pl.BlockSpec((1, N_HEADS, TM_FFN, DV), lambda i: (i // per_seq, 0, i % per_seq, 0))
    return pl.pallas_call(
        functools.partial(_outffn_kernel, final_norm=final_norm),
        grid=(m // TM_FFN,),
        in_specs=[tile, heads, _resident(wo.shape), _resident(g.shape), _resident(win.shape), _resident(wout.shape),
                  _resident(gf.shape)],
        out_specs=tile,
        out_shape=jax.ShapeDtypeStruct((m, D_MODEL), F32),
        scratch_shapes=[pltpu.VMEM((TM_FFN, D_MODEL), BF16), pltpu.VMEM((TM_FFN, D_FF), BF16)],
        compiler_params=_params(1),
        name="outffn",
    )(x2d, merged, wo, g, win, wout, gf)


def _inproj(x, g, wm, ws, nrm, cw, cb):
    bsz, seq, _ = x.shape
    tm = TM_FFN
    per_tile = tm // SUBLANES
    last = seq // SUBLANES - 1
    return pl.pallas_call(
        _inproj_kernel,
        grid=(bsz, seq // tm),
        in_specs=[pl.BlockSpec((1, tm, D_MODEL), lambda b, i: (b, i, 0)),
                  pl.BlockSpec((1, SUBLANES, D_MODEL), lambda b, i: (b, jnp.maximum(i * per_tile - 1, 0), 0)),
                  pl.BlockSpec((1, SUBLANES, D_MODEL), lambda b, i: (b, jnp.minimum((i + 1) * per_tile, last), 0)),
                  _resident(g.shape), _resident(wm.shape), _resident(ws.shape), _resident(nrm.shape),
                  _resident(cw.shape), _resident(cb.shape)],
        out_specs=[pl.BlockSpec((1, N_MAIN // DV, tm, DV), lambda b, i: (b, 0, i, 0)),
                   pl.BlockSpec((1, tm, LANES), lambda b, i: (b, i, 0))],
        out_shape=[jax.ShapeDtypeStruct((bsz, N_MAIN // DV, seq, DV), BF16),
                   jax.ShapeDtypeStruct((bsz, seq, LANES), F32)],
        scratch_shapes=[pltpu.VMEM((tm, D_MODEL), BF16)],
        compiler_params=_params(2),
        name="inproj",
    )(x, x, x, g, wm, ws, nrm, cw, cb)


def _mixer(um, sm, mp):
    bsz, _, seq, _ = um.shape
    nch = seq // CHUNK

    def group(name):
        base = GROUPS.index(name) * N_HEADS
        return pl.BlockSpec((1, 1, seq, DV), lambda b, h: (b, base + h, 0, 0))

    def head(arr):
        nd = arr.ndim
        return pl.BlockSpec((1,) + arr.shape[1:], lambda b, h: (h,) + (0,) * (nd - 1))

    plist = [mp[n] for n in ("w2", "b2", "selrow", "bifrow")]
    in_specs = [group(n) for n in GROUPS] + [pl.BlockSpec((1, seq, LANES), lambda b, h: (b, 0, 0))] \
        + [head(p) for p in plist]
    tok_bf = pltpu.VMEM((2, seq, DK), BF16)
    chunk_t_bf = pltpu.VMEM((2, nch, DK, CHUNK), BF16)
    chunk_tile = pltpu.VMEM((2, nch, SUBLANES, LANES), F32)
    slot_bf = pltpu.VMEM((PIPE_SLOTS, 2, CHUNK, CHUNK), BF16)
    slot_f32 = pltpu.VMEM((PIPE_SLOTS, 2, CHUNK, LANES), F32)
    slot_tile = pltpu.VMEM((PIPE_SLOTS, 2, SUBLANES, LANES), F32)
    return pl.pallas_call(
        _mixer_kernel,
        grid=(bsz, N_HEADS),
        in_specs=in_specs,
        out_specs=pl.BlockSpec((1, 1, seq, DV), lambda b, h: (b, h, 0, 0)),
        out_shape=jax.ShapeDtypeStruct((bsz, N_HEADS, seq, DV), BF16),
        scratch_shapes=[slot_f32,
                        tok_bf, tok_bf, chunk_t_bf,
                        pltpu.VMEM((2, nch, DK, LANES), F32),
                        pltpu.VMEM((nch, SUBLANES, CHUNK), F32),
                        pltpu.VMEM((nch, SUBLANES, CHUNK), F32),
                        pltpu.VMEM((2, seq, LANES), F32),
                        pltpu.VMEM((seq, CHUNK), F32),
                        pltpu.VMEM((nch, DK, CHUNK), BF16),
                        chunk_tile, chunk_tile, chunk_tile, chunk_tile,
                        slot_bf, slot_bf, slot_bf,
                        slot_f32, slot_f32,
                        slot_tile, slot_tile,
                        pltpu.VMEM((2, seq, DV), F32), pltpu.VMEM((2, seq, DV), F32),
                        pltpu.VMEM((2, DK, DV), F32), pltpu.VMEM((2, DK, DV), F32),
                        pltpu.VMEM((2, SUBLANES, DK), F32)],
        compiler_params=_params(2),
        name="mixer",
    )(*([um] * len(GROUPS)), sm, *plist)


def _prep_layer(w_in, gla_w2_fwd, gla_b2_fwd, gla_w2_bwd, gla_b2_bwd, gla_norm, ml_conv_w, ml_conv_b, ml_b_if,
                ml_norm):
    sizes = (512, 512, 1024, 1024, 32, 1024, 1024, 1024, 16, 1024, 1024)
    offs = [0]
    for s in sizes:
        offs.append(offs[-1] + s)
    gq, gk, gv, gr, glr, mqk, mv, mo, mif, gate_a, gate_b = [w_in[:, offs[i]:offs[i + 1]] for i in range(len(sizes))]

    def pair(a, b):
        hd = lambda w: w.reshape(D_MODEL, N_HEADS, DK)
        return jnp.concatenate([hd(a), hd(b)], axis=2).reshape(D_MODEL, N_HEADS * 2 * DK)

    parts = dict(gqk=pair(gq, gk), gv=gv, gr=gr, mqk=pair(mqk[:, :512], mqk[:, 512:]), mv=mv, mo=mo, ga=gate_a,
                 gb=gate_b)
    wm = jnp.concatenate([parts[n] for n in GROUPS[:PLAIN_GROUPS] + W_GATES], axis=1).astype(BF16)
    nrm = jnp.stack([gla_norm, ml_norm])
    ws = jnp.concatenate([glr, mif, jnp.zeros((D_MODEL, LANES - 48), F32)], axis=1).astype(BF16)

    hd = lambda a: a.reshape(a.shape[:-1] + (N_HEADS, DK))

    def w2pad(w2, row0):
        w = jnp.moveaxis(hd(w2), 1, 0)
        return jnp.pad(w, ((0, 0), (row0, LANES - row0 - GLA_LOWRANK), (0, 0)))

    w2 = jnp.concatenate([w2pad(gla_w2_fwd, SM_LR_F), w2pad(gla_w2_bwd, SM_LR_B)], axis=2).astype(BF16)
    b2 = jnp.concatenate([hd(gla_b2_fwd), hd(gla_b2_bwd)], axis=1)[:, None, :]

    g_idx = jnp.arange(4)
    h_idx = jnp.arange(N_HEADS)
    src = SM_IF + 4 * g_idx[None, :] + h_idx[:, None]
    selrow = jnp.zeros((N_HEADS, SUBLANES, LANES), F32).at[h_idx[:, None], g_idx[None, :], src].set(1.0)
    bif = jnp.pad(ml_b_if.reshape(4, N_HEADS).T, ((0, 0), (0, SUBLANES - 4)))
    bifrow = jnp.broadcast_to(bif[:, :, None], (N_HEADS, SUBLANES, GATE_ROWS))

    cw = jnp.concatenate([hd(ml_conv_w[:, :512]), hd(ml_conv_w[:, 512:])], axis=2).reshape(3, GROUP_W)
    cb = jnp.concatenate([hd(ml_conv_b[:512]), hd(ml_conv_b[512:])], axis=1).reshape(1, GROUP_W)
    mp = dict(w2=w2, b2=b2, selrow=selrow.astype(BF16), bifrow=bifrow)
    return wm, ws, nrm, cw, cb, mp


def kernel(x_prompt, x_sample, g_ffn1, w_ffn1_in, w_ffn1_out, g_mix, w_in, gla_w2_fwd, gla_b2_fwd, gla_w2_bwd,
           gla_b2_bwd, gla_norm, ml_conv_w, ml_conv_b, ml_b_if, ml_norm, w_out, g_ffn2, w_ffn2_in, w_ffn2_out,
           g_final):
    depth = w_in.shape[0]
    layers = []
    for l in range(depth):
        layers.append(dict(
            g1=g_ffn1[l][None, :], w1i=w_ffn1_in[l].astype(BF16), w1o=w_ffn1_out[l].astype(BF16),
            gm=g_mix[l][None, :],
            mix=_prep_layer(w_in[l], gla_w2_fwd[l], gla_b2_fwd[l], gla_w2_bwd[l], gla_b2_bwd[l], gla_norm[l],
                            ml_conv_w[l], ml_conv_b[l], ml_b_if[l], ml_norm[l]),
            wo=w_out[l].astype(BF16), g2=g_ffn2[l][None, :], w2i=w_ffn2_in[l].astype(BF16),
            w2o=w_ffn2_out[l].astype(BF16)))
    gf = g_final[None, :]

    def trunk(x):
        bsz, seq, _ = x.shape
        for l, p in enumerate(layers):
            wm, ws, nrm, cw, cb, mp = p["mix"]
            x1 = _ffn1(x.reshape(bsz * seq, D_MODEL), p["g1"], p["w1i"], p["w1o"])
            um, sm = _inproj(x1.reshape(bsz, seq, D_MODEL), p["gm"], wm, ws, nrm, cw, cb)
            merged = _mixer(um, sm, mp)
            x = _outffn(x1, merged, p["wo"], p["g2"], p["w2i"], p["w2o"], gf,
                        final_norm=(l == depth - 1))
            x = x.reshape(bsz, seq, D_MODEL)
        return x

    return trunk(x_prompt), trunk(x_sample)
```

```python
import functools

import jax
import jax.numpy as jnp
from jax import lax
from jax.experimental import pallas as pl
from jax.experimental.pallas import tpu as pltpu

F32 = jnp.float32
BF16 = jnp.bfloat16

D_MODEL = 1024
D_FF = 2816
N_HEADS = 4
DK = 128
DV = 256
GLA_LOWRANK = 16
GLA_TAU = 16.0
EPS = 1e-6
QK_SCALE = DK ** -0.5
LOG2E = 1.4426950408889634

LANES = 128
SUBLANES = 8
CHUNK = 128
GLA_MID = CHUNK // 2
TM_FFN = 512
TF = 256
TN_IN = 256
PAR_UNROLL = 4
GATE_ROWS = 512
GLA_GROUP = 4
SCAN_GROUP = 2
PIPE_SLOTS = 2 * max(GLA_GROUP, SCAN_GROUP)
VMEM_LIMIT = 56 * 1024 * 1024

GROUP_W = N_HEADS * DV
GROUPS = ("gqk", "gv", "mqk", "mv", "wa", "wb")
PLAIN_GROUPS = 4
W_GATES = ("gr", "ga", "mo", "gb")
N_MAIN = len(GROUPS) * GROUP_W
SM_LR_F, SM_LR_B, SM_IF = 0, GLA_LOWRANK, 2 * GLA_LOWRANK


def _dot(a, b):
    return jnp.dot(a, b, preferred_element_type=F32)


def _dot_nt(a, b):
    return lax.dot_general(a, b, (((1,), (1,)), ((), ())), preferred_element_type=F32)


def _rms(x, g):
    return x * lax.rsqrt(jnp.mean(x * x, axis=-1, keepdims=True) + EPS) * g


def _logsig(z):
    return jnp.minimum(z, 0.0) - jnp.log(1.0 + jnp.exp(-jnp.abs(z)))


def _split_hi_lo(x):
    hi = x.astype(BF16)
    lo = (x - hi.astype(F32)).astype(BF16)
    return hi, lo


def _sel_dot(sel_bf16, x):
    hi, lo = _split_hi_lo(x)
    return _dot(sel_bf16, hi) + _dot(sel_bf16, lo)


def _dot_sel(x, sel_bf16):
    hi, lo = _split_hi_lo(x)
    return _dot(hi, sel_bf16) + _dot(lo, sel_bf16)


def _tile(x):
    return jnp.broadcast_to(x, (SUBLANES, LANES))


def _swiglu_into(h_scr, win_ref, wout_ref, act_scr):
    for j in range(D_FF // TF):
        a = _dot(h_scr[...], win_ref[:, j * TF:(j + 1) * TF])
        g = _dot(h_scr[...], win_ref[:, D_FF + j * TF:D_FF + (j + 1) * TF])
        act_scr[:, j * TF:(j + 1) * TF] = (a * jax.nn.sigmoid(a) * g).astype(BF16)
    return _dot(act_scr[...], wout_ref[...])


def _ffn1_kernel(x_ref, g_ref, win_ref, wout_ref, o_ref, h_scr, act_scr):
    x = x_ref[...]
    h_scr[...] = _rms(x, g_ref[...]).astype(BF16)
    o_ref[...] = x + 0.5 * _swiglu_into(h_scr, win_ref, wout_ref, act_scr)


def _outffn_kernel(x_ref, m_ref, wo_ref, g_ref, win_ref, wout_ref, gf_ref, o_ref, h_scr, act_scr, *, final_norm):
    x = x_ref[...]
    for hd in range(N_HEADS):
        x = x + _dot(m_ref[0, hd], wo_ref[hd * DV:(hd + 1) * DV, :])
    h_scr[...] = _rms(x, g_ref[...]).astype(BF16)
    x = x + 0.5 * _swiglu_into(h_scr, win_ref, wout_ref, act_scr)
    o_ref[...] = _rms(x, gf_ref[...]) if final_norm else x


def _inproj_kernel(x_ref, xp_ref, xn_ref, g_ref, wm_ref, ws_ref, nrm_ref, cw_ref, cb_ref, um_ref, sm_ref, h_scr):
    tm = x_ref.shape[1]
    h_scr[...] = _rms(x_ref[0], g_ref[...]).astype(BF16)
    i = pl.program_id(1)
    r16 = lax.broadcasted_iota(jnp.int32, (2 * SUBLANES, 1), 0)
    inside = jnp.where(r16 < SUBLANES, jnp.where(i > 0, 1.0, 0.0), jnp.where(i < pl.num_programs(1) - 1, 1.0, 0.0))
    h_halo = _rms(jnp.concatenate([xp_ref[0], xn_ref[0]], axis=0), g_ref[...]).astype(BF16)
    lane_q = lax.broadcasted_iota(jnp.int32, (1, TN_IN), 1) % (2 * DK) < DK
    q_scale = jnp.where(lane_q, QK_SCALE, 1.0)
    def store(col0, val):
        for k in range(TN_IN // DV):
            um_ref[0, col0 // DV + k] = val[:, k * DV:(k + 1) * DV].astype(BF16)

    def plain(name, j):
        cols = slice(GROUPS.index(name) * GROUP_W + j * TN_IN, GROUPS.index(name) * GROUP_W + (j + 1) * TN_IN)
        store(cols.start, _dot(h_scr[...], wm_ref[:, cols]))

    def conv(name, j):
        cols = slice(GROUPS.index(name) * GROUP_W + j * TN_IN, GROUPS.index(name) * GROUP_W + (j + 1) * TN_IN)
        cc = slice(j * TN_IN, (j + 1) * TN_IN)
        u = _dot(h_scr[...], wm_ref[:, cols])
        halo = _dot(h_halo, wm_ref[:, cols]) * inside
        xe = jnp.concatenate([halo[:SUBLANES], u, halo[SUBLANES:]], axis=0)
        prev = pltpu.roll(xe, 1, 0)[SUBLANES:SUBLANES + tm]
        nxt = pltpu.roll(xe, tm + 2 * SUBLANES - 1, 0)[SUBLANES:SUBLANES + tm]
        cv = cb_ref[:, cc] + prev * cw_ref[0:1, cc] + u * cw_ref[1:2, cc] + nxt * cw_ref[2:3, cc]
        store(cols.start, cv * jax.nn.sigmoid(cv) * q_scale)

    def gates(br, j):
        out_gate, branch_gate = ((PLAIN_GROUPS + 2 * br + t) * GROUP_W + j * TN_IN for t in (0, 1))
        u = _dot(h_scr[...], wm_ref[:, out_gate:out_gate + TN_IN])
        out_act = u * jax.nn.sigmoid(u) if br == 0 else jax.nn.sigmoid(u)
        w = out_act * jax.nn.sigmoid(_dot(h_scr[...], wm_ref[:, branch_gate:branch_gate + TN_IN]))
        dst = (PLAIN_GROUPS + br) * GROUP_W + j * TN_IN
        store(dst, w * nrm_ref[br:br + 1, j * TN_IN:(j + 1) * TN_IN])

    n_j = GROUP_W // TN_IN
    heavy_tasks = [t for j in range(n_j) for t in ((conv, "mqk", j), (gates, 0, j), (gates, 1, j))]
    light_tasks = [(name, j) for j in range(n_j) for name in ("gqk", "gv", "mv")]
    for heavy, light in zip(heavy_tasks, light_tasks):
        heavy[0](*heavy[1:])
        plain(*light)
    sm_ref[0] = _dot(h_scr[...], ws_ref[...])


def _mixer_kernel(gqk_ref, gv_ref, mqk_ref, mv_ref, wa_ref, wb_ref, sm_ref,
                  w2_ref, b2_ref, selrow_ref, bifrow_ref,
                  out_ref,
                  bcum_scr, att_scr, qin_scr, kst_t_scr, dcol_scr, gt_scr, grow_scr, bt_scr, qkt_scr, kt_scr,
                  a_scr, g_scr, mprev_scr, mnew_scr,
                  sml_scr, qw_scr, ks_t_scr, floor_scr, dint_scr, dec_scr, dn_scr, og_scr, hm_scr, s_scr, c_scr, n_scr,
                  sbf_scr, cbf_scr):
    gqk_ref, gv_ref, mqk_ref, mv_ref, wa_ref, wb_ref, out_ref = (
        r.at[0] for r in (gqk_ref, gv_ref, mqk_ref, mv_ref, wa_ref, wb_ref, out_ref))
    seq = gqk_ref.shape[1]
    nch = seq // CHUNK
    ri = lax.broadcasted_iota(jnp.int32, (CHUNK, CHUNK), 0)
    ci = lax.broadcasted_iota(jnp.int32, (CHUNK, CHUNK), 1)
    lower = ci <= ri
    upper = ci >= ri
    eye = ci == ri
    lower_bf = lower.astype(BF16)
    upper_bf = upper.astype(BF16)
    dirs = ((lower, GLA_MID - 1, CHUNK - 1), (upper, GLA_MID, 0))

    def chunk_rows(c):
        return pl.ds(pl.multiple_of(c * CHUNK, CHUNK), CHUNK)

    def to_col(row):
        return jnp.sum(jnp.where(eye, row, 0.0), axis=1, keepdims=True)

    def act_body(j, carry):
        rows = pl.ds(pl.multiple_of(j * GATE_ROWS, GATE_ROWS), GATE_ROWS)
        sm = sm_ref[0, rows, :]
        hm_scr[0, rows, :] = _logsig(_dot(sm.astype(BF16), w2_ref[0]) + b2_ref[0]) * (LOG2E / GLA_TAU)
        hi, lo = _split_hi_lo(sm)
        g_t = _dot_nt(selrow_ref[0], hi) + _dot_nt(selrow_ref[0], lo) + bifrow_ref[0]
        is_forget = lax.broadcasted_iota(jnp.int32, (SUBLANES, GATE_ROWS), 0) % 2 == 1
        g_t = jnp.where(is_forget, _logsig(g_t), g_t) * LOG2E
        for jj in range(GATE_ROWS // CHUNK):
            gt_scr[j * (GATE_ROWS // CHUNK) + jj] = g_t[:, jj * CHUNK:(jj + 1) * CHUNK]
        return carry

    lax.fori_loop(0, seq // GATE_ROWS, act_body, 0)

    row8 = lax.broadcasted_iota(jnp.int32, (SUBLANES, CHUNK), 0)

    def gate_chunk(c, slot):
        rows = chunk_rows(c)
        bcum_scr[slot, 0] = _sel_dot(lower_bf, hm_scr[0, rows, :DK])
        bcum_scr[slot, 1] = _sel_dot(upper_bf, hm_scr[0, rows, DK:])
        r = gt_scr[c]
        cum_f = _dot_sel(r, upper_bf)
        cum_b = _dot_sel(r, lower_bf)
        i_f, b_f, i_b, b_b = r[0:1], cum_f[1:2], r[2:3], cum_b[3:4]
        c_f = i_f - b_f
        c_b = i_b - b_b
        g8 = jnp.zeros((SUBLANES, CHUNK), F32)
        for idx, val in enumerate((i_f, b_f, i_b, b_b, c_f, c_b)):
            g8 = jnp.where(row8 == idx, val, g8)
        grow_scr[c] = g8
        for d, b, cc in ((0, b_f, c_f), (1, b_b, c_b)):
            l_idx = dirs[d][2]
            a = b[:, l_idx:l_idx + 1]
            a_scr[d, c] = _tile(a)
            g_scr[d, c] = _tile(a + jnp.max(cc, axis=1, keepdims=True))
            bt_scr[d, rows, :] = jnp.broadcast_to(to_col(b), (CHUNK, LANES))

    def gla_chunk(c, slot):
        rows = chunk_rows(c)
        qk = gqk_ref[0, rows, :].astype(F32)
        q = qk[:, :DK] * QK_SCALE
        k = qk[:, DK:]
        for d in (0, 1):
            mask, r_idx, l_idx = dirs[d]
            b = bcum_scr[slot, d]
            rho = b[r_idx:r_idx + 1, :]
            b_last = b[l_idx:l_idx + 1, :]
            q_mid = (q * jnp.exp2(b - rho)).astype(BF16)
            k_mid = (k * jnp.exp2(rho - b)).astype(BF16)
            att_scr[d, rows, :] = jnp.where(mask, _dot_nt(q_mid, k_mid), 0.0).astype(BF16)
            qin_scr[d, rows, :] = (q * jnp.exp2(b)).astype(BF16)
            kst_t_scr[d, c] = (k * jnp.exp2(b_last - b)).astype(BF16).T
            dcol_scr[d, c] = jnp.broadcast_to(to_col(jnp.exp2(b_last)), (DK, LANES))
        mqk = mqk_ref[0, rows, :]
        qkt_scr[rows, :] = _dot_nt(mqk[:, :DK], mqk[:, DK:])
        kt_scr[c] = mqk[:, DK:].T

    def pipelined(produce, consume, group):
        for j in range(group):
            produce(j, j)

        def body(q, carry):
            for half in (0, 1):
                base = (2 * q + half) * group
                for j in range(group):
                    produce(jnp.minimum(base + group + j, nch - 1), (1 - half) * group + j)
                for j in range(group):
                    consume(base + j, half * group + j)
            return carry

        lax.fori_loop(0, nch // (2 * group), body, 0)

    pipelined(gate_chunk, gla_chunk, GLA_GROUP)

    def m_body(i, carry):
        m_f, m_b = carry
        cb = nch - 1 - i
        mprev_scr[0, i] = m_f
        mprev_scr[1, cb] = m_b
        m_f = jnp.maximum(a_scr[0, i] + m_f, g_scr[0, i])
        m_b = jnp.maximum(a_scr[1, cb] + m_b, g_scr[1, cb])
        mnew_scr[0, i] = m_f
        mnew_scr[1, cb] = m_b
        return m_f, m_b

    m0 = jnp.full((SUBLANES, LANES), -jnp.inf, F32)
    lax.fori_loop(0, nch, m_body, (m0, m0))

    def prep_dir(d, c, slot):
        rows = chunk_rows(c)
        mask = dirs[d][0]
        g8 = grow_scr[c]
        b_t = bt_scr[d, rows, :]
        c_s = g8[4 + d:5 + d, :]
        m_prev = mprev_scr[d, c][0:1, 0:1]
        m_new = mnew_scr[d, c][0:1, 0:1]
        b_last = a_scr[d, c][0:1, 0:1]
        log_d = jnp.where(mask, b_t + c_s, -jnp.inf)
        m_inter = b_t + m_prev
        m_t = jnp.maximum(m_inter, jnp.max(log_d, axis=1, keepdims=True))
        sml = qkt_scr[rows, :] * jnp.exp2(log_d - m_t)
        sml_bf = sml.astype(BF16)
        sml_scr[slot, d] = sml_bf
        dint_scr[slot, d] = _dot(sml_bf, jnp.ones((CHUNK, LANES), BF16))
        qw_scr[slot, d] = (jnp.exp2(m_inter - m_t) * mqk_ref[0, rows, :DK].astype(F32)).astype(BF16)
        floor_scr[slot, d] = jnp.exp2(-m_t)
        w_st = jnp.broadcast_to(jnp.exp2(b_last + c_s - m_new), (SUBLANES, CHUNK))
        ks_t_scr[slot, d] = (kt_scr[c].astype(F32) * w_st[0:1, :]).astype(BF16)
        dn_scr[slot, d] = _dot(w_st.astype(BF16), mqk_ref[0, rows, DK:])
        dec_scr[slot, d] = _tile(jnp.exp2(b_last + m_prev - m_new))

    def scan_dir(d, c, slot):
        rows = chunk_rows(c)
        v = gv_ref[0, rows, :]
        og_scr[d, rows, :] = _dot(jnp.concatenate([att_scr[d, rows, :], qin_scr[d, rows, :]], axis=1),
                                  jnp.concatenate([v, sbf_scr[d]], axis=0))
        dcol = dcol_scr[d, c]
        s_new = s_scr[d] * jnp.concatenate([dcol, dcol], axis=1) + _dot(kst_t_scr[d, c], v)
        s_scr[d] = s_new
        sbf_scr[d] = s_new.astype(BF16)
        mv = mv_ref[0, rows, :]
        nst = n_scr[d]
        qw = qw_scr[slot, d]
        num = _dot(jnp.concatenate([sml_scr[slot, d], qw], axis=1), jnp.concatenate([mv, cbf_scr[d]], axis=0))
        den = dint_scr[slot, d] + jnp.sum(qw.astype(F32) * nst[0:1, :], axis=1, keepdims=True)
        rden = 1.0 / jnp.maximum(jnp.abs(den), floor_scr[slot, d])
        hm_scr[d, rows, :] = num * jnp.concatenate([rden, rden], axis=1)
        decay = dec_scr[slot, d][0:1, 0:1]
        c_new = decay * c_scr[d] + _dot(ks_t_scr[slot, d], mv)
        c_scr[d] = c_new
        cbf_scr[d] = c_new.astype(BF16)
        n_scr[d] = decay * nst + dn_scr[slot, d]

    for state in (s_scr, sbf_scr, c_scr, cbf_scr, n_scr):
        state[...] = jnp.zeros_like(state)

    def prep_step(i, slot):
        prep_dir(0, i, slot)
        prep_dir(1, nch - 1 - i, slot)

    def scan_step(i, slot):
        scan_dir(0, i, slot)
        scan_dir(1, nch - 1 - i, slot)

    pipelined(prep_step, scan_step, SCAN_GROUP)

    def merge_body(c, carry):
        rows = chunk_rows(c)
        og = og_scr[0, rows, :] + og_scr[1, rows, :]
        hm = hm_scr[0, rows, :] + hm_scr[1, rows, :]
        o_a = og * lax.rsqrt(jnp.mean(og * og, axis=-1, keepdims=True) + EPS)
        o_b = hm * lax.rsqrt(jnp.mean(hm * hm, axis=-1, keepdims=True) + EPS)
        merged = o_a * wa_ref[0, rows, :].astype(F32) + o_b * wb_ref[0, rows, :].astype(F32)
        out_ref[0, rows, :] = merged.astype(BF16)
        return carry

    lax.fori_loop(0, nch, merge_body, 0, unroll=PAR_UNROLL)


def _resident(shape):
    nd = len(shape)
    return pl.BlockSpec(shape, lambda *_: (0,) * nd, pipeline_mode=pl.Buffered(1))


def _params(n_axes):
    return pltpu.CompilerParams(dimension_semantics=("arbitrary",) * n_axes, vmem_limit_bytes=VMEM_LIMIT)


def _ffn1(x2d, g, win, wout):
    m = x2d.shape[0]
    tile = pl.BlockSpec((TM_FFN, D_MODEL), lambda i: (i, 0))
    return pl.pallas_call(
        _ffn1_kernel,
        grid=(m // TM_FFN,),
        in_specs=[tile, _resident(g.shape), _resident(win.shape), _resident(wout.shape)],
        out_specs=tile,
        out_shape=jax.ShapeDtypeStruct((m, D_MODEL), F32),
        scratch_shapes=[pltpu.VMEM((TM_FFN, D_MODEL), BF16), pltpu.VMEM((TM_FFN, D_FF), BF16)],
        compiler_params=_params(1),
        name="ffn1",
    )(x2d, g, win, wout)


def _outffn(x2d, merged, wo, g, win, wout, gf, final_norm):
    m = x2d.shape[0]
    per_seq = merged.shape[2] // TM_FFN
    tile = pl.BlockSpec((TM_FFN, D_MODEL), lambda i: (i, 0))
    heads = pl.BlockSpec((1, N_HEADS, TM_FFN, DV), lambda i: (i // per_seq, 0, i % per_seq, 0))
    return pl.pallas_call(
        functools.partial(_outffn_kernel, final_norm=final_norm),
        grid=(m // TM_FFN,),
        in_specs=[tile, heads, _resident(wo.shape), _resident(g.shape), _resident(win.shape), _resident(wout.shape),
                  _resident(gf.shape)],
        out_specs=tile,
        out_shape=jax.ShapeDtypeStruct((m, D_MODEL), F32),
        scratch_shapes=[pltpu.VMEM((TM_FFN, D_MODEL), BF16), pltpu.VMEM((TM_FFN, D_FF), BF16)],
        compiler_params=_params(1),
        name="outffn",
    )(x2d, merged, wo, g, win, wout, gf)


def _inproj(x, g, wm, ws, nrm, cw, cb):
    bsz, seq, _ = x.shape
    tm = TM_FFN
    per_tile = tm // SUBLANES
    last = seq // SUBLANES - 1
    return pl.pallas_call(
        _inproj_kernel,
        grid=(bsz, seq // tm),
        in_specs=[pl.BlockSpec((1, tm, D_MODEL), lambda b, i: (b, i, 0)),
                  pl.BlockSpec((1, SUBLANES, D_MODEL), lambda b, i: (b, jnp.maximum(i * per_tile - 1, 0), 0)),
                  pl.BlockSpec((1, SUBLANES, D_MODEL), lambda b, i: (b, jnp.minimum((i + 1) * per_tile, last), 0)),
                  _resident(g.shape), _resident(wm.shape), _resident(ws.shape), _resident(nrm.shape),
                  _resident(cw.shape), _resident(cb.shape)],
        out_specs=[pl.BlockSpec((1, N_MAIN // DV, tm, DV), lambda b, i: (b, 0, i, 0)),
                   pl.BlockSpec((1, tm, LANES), lambda b, i: (b, i, 0))],
        out_shape=[jax.ShapeDtypeStruct((bsz, N_MAIN // DV, seq, DV), BF16),
                   jax.ShapeDtypeStruct((bsz, seq, LANES), F32)],
        scratch_shapes=[pltpu.VMEM((tm, D_MODEL), BF16)],
        compiler_params=_params(2),
        name="inproj",
    )(x, x, x, g, wm, ws, nrm, cw, cb)


def _mixer(um, sm, mp):
    bsz, _, seq, _ = um.shape
    nch = seq // CHUNK

    def group(name):
        base = GROUPS.index(name) * N_HEADS
        return pl.BlockSpec((1, 1, seq, DV), lambda b, h: (b, base + h, 0, 0))

    def head(arr):
        nd = arr.ndim
        return pl.BlockSpec((1,) + arr.shape[1:], lambda b, h: (h,) + (0,) * (nd - 1))

    plist = [mp[n] for n in ("w2", "b2", "selrow", "bifrow")]
    in_specs = [group(n) for n in GROUPS] + [pl.BlockSpec((1, seq, LANES), lambda b, h: (b, 0, 0))] \
        + [head(p) for p in plist]
    tok_bf = pltpu.VMEM((2, seq, DK), BF16)
    chunk_t_bf = pltpu.VMEM((2, nch, DK, CHUNK), BF16)
    chunk_tile = pltpu.VMEM((2, nch, SUBLANES, LANES), F32)
    slot_bf = pltpu.VMEM((PIPE_SLOTS, 2, CHUNK, CHUNK), BF16)
    slot_f32 = pltpu.VMEM((PIPE_SLOTS, 2, CHUNK, LANES), F32)
    slot_tile = pltpu.VMEM((PIPE_SLOTS, 2, SUBLANES, LANES), F32)
    return pl.pallas_call(
        _mixer_kernel,
        grid=(bsz, N_HEADS),
        in_specs=in_specs,
        out_specs=pl.BlockSpec((1, 1, seq, DV), lambda b, h: (b, h, 0, 0)),
        out_shape=jax.ShapeDtypeStruct((bsz, N_HEADS, seq, DV), BF16),
        scratch_shapes=[slot_f32,
                        tok_bf, tok_bf, chunk_t_bf,
                        pltpu.VMEM((2, nch, DK, LANES), F32),
                        pltpu.VMEM((nch, SUBLANES, CHUNK), F32),
                        pltpu.VMEM((nch, SUBLANES, CHUNK), F32),
                        pltpu.VMEM((2, seq, LANES), F32),
                        pltpu.VMEM((seq, CHUNK), F32),
                        pltpu.VMEM((nch, DK, CHUNK), BF16),
                        chunk_tile, chunk_tile, chunk_tile, chunk_tile,
                        slot_bf, slot_bf, slot_bf,
                        slot_f32, slot_f32,
                        slot_tile, slot_tile,
                        pltpu.VMEM((2, seq, DV), F32), pltpu.VMEM((2, seq, DV), F32),
                        pltpu.VMEM((2, DK, DV), F32), pltpu.VMEM((2, DK, DV), F32),
                        pltpu.VMEM((2, SUBLANES, DK), F32),
                        pltpu.VMEM((2, DK, DV), BF16), pltpu.VMEM((2, DK, DV), BF16)],
        compiler_params=_params(2),
        name="mixer",
    )(*([um] * len(GROUPS)), sm, *plist)


def _prep_layer(w_in, gla_w2_fwd, gla_b2_fwd, gla_w2_bwd, gla_b2_bwd, gla_norm, ml_conv_w, ml_conv_b, ml_b_if,
                ml_norm):
    sizes = (512, 512, 1024, 1024, 32, 1024, 1024, 1024, 16, 1024, 1024)
    offs = [0]
    for s in sizes:
        offs.append(offs[-1] + s)
    gq, gk, gv, gr, glr, mqk, mv, mo, mif, gate_a, gate_b = [w_in[:, offs[i]:offs[i + 1]] for i in range(len(sizes))]

    def pair(a, b):
        hd = lambda w: w.reshape(D_MODEL, N_HEADS, DK)
        return jnp.concatenate([hd(a), hd(b)], axis=2).reshape(D_MODEL, N_HEADS * 2 * DK)

    parts = dict(gqk=pair(gq, gk), gv=gv, gr=gr, mqk=pair(mqk[:, :512], mqk[:, 512:]), mv=mv, mo=mo, ga=gate_a,
                 gb=gate_b)
    wm = jnp.concatenate([parts[n] for n in GROUPS[:PLAIN_GROUPS] + W_GATES], axis=1).astype(BF16)
    nrm = jnp.stack([gla_norm, ml_norm])
    ws = jnp.concatenate([glr, mif, jnp.zeros((D_MODEL, LANES - 48), F32)], axis=1).astype(BF16)

    hd = lambda a: a.reshape(a.shape[:-1] + (N_HEADS, DK))

    def w2pad(w2, row0):
        w = jnp.moveaxis(hd(w2), 1, 0)
        return jnp.pad(w, ((0, 0), (row0, LANES - row0 - GLA_LOWRANK), (0, 0)))

    w2 = jnp.concatenate([w2pad(gla_w2_fwd, SM_LR_F), w2pad(gla_w2_bwd, SM_LR_B)], axis=2).astype(BF16)
    b2 = jnp.concatenate([hd(gla_b2_fwd), hd(gla_b2_bwd)], axis=1)[:, None, :]

    g_idx = jnp.arange(4)
    h_idx = jnp.arange(N_HEADS)
    src = SM_IF + 4 * g_idx[None, :] + h_idx[:, None]
    selrow = jnp.zeros((N_HEADS, SUBLANES, LANES), F32).at[h_idx[:, None], g_idx[None, :], src].set(1.0)
    bif = jnp.pad(ml_b_if.reshape(4, N_HEADS).T, ((0, 0), (0, SUBLANES - 4)))
    bifrow = jnp.broadcast_to(bif[:, :, None], (N_HEADS, SUBLANES, GATE_ROWS))

    cw = jnp.concatenate([hd(ml_conv_w[:, :512]), hd(ml_conv_w[:, 512:])], axis=2).reshape(3, GROUP_W)
    cb = jnp.concatenate([hd(ml_conv_b[:512]), hd(ml_conv_b[512:])], axis=1).reshape(1, GROUP_W)
    mp = dict(w2=w2, b2=b2, selrow=selrow.astype(BF16), bifrow=bifrow)
    return wm, ws, nrm, cw, cb, mp


def kernel(x_prompt, x_sample, g_ffn1, w_ffn1_in, w_ffn1_out, g_mix, w_in, gla_w2_fwd, gla_b2_fwd, gla_w2_bwd,
           gla_b2_bwd, gla_norm, ml_conv_w, ml_conv_b, ml_b_if, ml_norm, w_out, g_ffn2, w_ffn2_in, w_ffn2_out,
           g_final):
    depth = w_in.shape[0]
    layers = []
    for l in range(depth):
        layers.append(dict(
            g1=g_ffn1[l][None, :], w1i=w_ffn1_in[l].astype(BF16), w1o=w_ffn1_out[l].astype(BF16),
            gm=g_mix[l][None, :],
            mix=_prep_layer(w_in[l], gla_w2_fwd[l], gla_b2_fwd[l], gla_w2_bwd[l], gla_b2_bwd[l], gla_norm[l],
                            ml_conv_w[l], ml_conv_b[l], ml_b_if[l], ml_norm[l]),
            wo=w_out[l].astype(BF16), g2=g_ffn2[l][None, :], w2i=w_ffn2_in[l].astype(BF16),
            w2o=w_ffn2_out[l].astype(BF16)))
    gf = g_final[None, :]

    def trunk(x):
        bsz, seq, _ = x.shape
        for l, p in enumerate(layers):
            wm, ws, nrm, cw, cb, mp = p["mix"]
            x1 = _ffn1(x.reshape(bsz * seq, D_MODEL), p["g1"], p["w1i"], p["w1o"])
            um, sm = _inproj(x1.reshape(bsz, seq, D_MODEL), p["gm"], wm, ws, nrm, cw, cb)
            merged = _mixer(um, sm, mp)
            x = _outffn(x1, merged, p["wo"], p["g2"], p["w2i"], p["w2o"], gf,
                        final_norm=(l == depth - 1))
            x = x.reshape(bsz, seq, D_MODEL)
        return x

    return trunk(x_prompt), trunk(x_sample)
```

```python
import functools

import jax
import jax.numpy as jnp
from jax import lax
from jax.experimental import pallas as pl
from jax.experimental.pallas import tpu as pltpu

F32 = jnp.float32
BF16 = jnp.bfloat16

D_MODEL = 1024
D_FF = 2816
N_HEADS = 4
DK = 128
DV = 256
GLA_LOWRANK = 16
GLA_TAU = 16.0
EPS = 1e-6
QK_SCALE = DK ** -0.5
LOG2E = 1.4426950408889634

LANES = 128
SUBLANES = 8
CHUNK = 128
GLA_MID = CHUNK // 2
TM_FFN = 512
TF = 256
TN_IN = 256
PAR_UNROLL = 8
GATE_ROWS = 512
GLA_GROUP = 4
SCAN_GROUP = 2
PIPE_SLOTS = 2 * max(GLA_GROUP, SCAN_GROUP)
VMEM_LIMIT = 56 * 1024 * 1024

GROUP_W = N_HEADS * DV
GROUPS = ("gqk", "gv", "mqk", "mv", "wa", "wb")
PLAIN_GROUPS = 4
W_GATES = ("gr", "ga", "mo", "gb")
N_MAIN = len(GROUPS) * GROUP_W
SM_LR_F, SM_LR_B, SM_IF = 0, GLA_LOWRANK, 2 * GLA_LOWRANK


def _dot(a, b):
    return jnp.dot(a, b, preferred_element_type=F32)


def _dot_nt(a, b):
    return lax.dot_general(a, b, (((1,), (1,)), ((), ())), preferred_element_type=F32)


def _rms(x, g):
    return x * lax.rsqrt(jnp.mean(x * x, axis=-1, keepdims=True) + EPS) * g


def _logsig(z):
    return jnp.minimum(z, 0.0) - jnp.log(1.0 + jnp.exp(-jnp.abs(z)))


def _split_hi_lo(x):
    hi = x.astype(BF16)
    lo = (x - hi.astype(F32)).astype(BF16)
    return hi, lo


def _sel_dot(sel_bf16, x):
    hi, lo = _split_hi_lo(x)
    return _dot(sel_bf16, hi) + _dot(sel_bf16, lo)


def _dot_sel(x, sel_bf16):
    hi, lo = _split_hi_lo(x)
    return _dot(hi, sel_bf16) + _dot(lo, sel_bf16)


def _tile(x):
    return jnp.broadcast_to(x, (SUBLANES, LANES))


def _swiglu_into(h_scr, win_ref, wout_ref, act_scr):
    for j in range(D_FF // TF):
        a = _dot(h_scr[...], win_ref[:, j * TF:(j + 1) * TF])
        g = _dot(h_scr[...], win_ref[:, D_FF + j * TF:D_FF + (j + 1) * TF])
        act_scr[:, j * TF:(j + 1) * TF] = (a * jax.nn.sigmoid(a) * g).astype(BF16)
    return _dot(act_scr[...], wout_ref[...])


def _ffn1_kernel(x_ref, g_ref, win_ref, wout_ref, o_ref, h_scr, act_scr):
    x = x_ref[...]
    h_scr[...] = _rms(x, g_ref[...]).astype(BF16)
    o_ref[...] = x + 0.5 * _swiglu_into(h_scr, win_ref, wout_ref, act_scr)


def _outffn_kernel(x_ref, m_ref, wo_ref, g_ref, win_ref, wout_ref, gf_ref, o_ref, h_scr, act_scr, *, final_norm):
    x = x_ref[...]
    for hd in range(N_HEADS):
        x = x + _dot(m_ref[0, hd], wo_ref[hd * DV:(hd + 1) * DV, :])
    h_scr[...] = _rms(x, g_ref[...]).astype(BF16)
    x = x + 0.5 * _swiglu_into(h_scr, win_ref, wout_ref, act_scr)
    o_ref[...] = _rms(x, gf_ref[...]) if final_norm else x


def _inproj_kernel(x_ref, xp_ref, xn_ref, g_ref, wm_ref, ws_ref, nrm_ref, cw_ref, cb_ref, um_ref, sm_ref, h_scr):
    tm = x_ref.shape[1]
    h_scr[...] = _rms(x_ref[0], g_ref[...]).astype(BF16)
    i = pl.program_id(1)
    r16 = lax.broadcasted_iota(jnp.int32, (2 * SUBLANES, 1), 0)
    inside = jnp.where(r16 < SUBLANES, jnp.where(i > 0, 1.0, 0.0), jnp.where(i < pl.num_programs(1) - 1, 1.0, 0.0))
    h_halo = _rms(jnp.concatenate([xp_ref[0], xn_ref[0]], axis=0), g_ref[...]).astype(BF16)
    lane_q = lax.broadcasted_iota(jnp.int32, (1, TN_IN), 1) % (2 * DK) < DK
    q_scale = jnp.where(lane_q, QK_SCALE, 1.0)
    def store(col0, val):
        for k in range(TN_IN // DV):
            um_ref[0, col0 // DV + k] = val[:, k * DV:(k + 1) * DV].astype(BF16)

    def plain(name, j):
        cols = slice(GROUPS.index(name) * GROUP_W + j * TN_IN, GROUPS.index(name) * GROUP_W + (j + 1) * TN_IN)
        store(cols.start, _dot(h_scr[...], wm_ref[:, cols]))

    def conv(name, j):
        cols = slice(GROUPS.index(name) * GROUP_W + j * TN_IN, GROUPS.index(name) * GROUP_W + (j + 1) * TN_IN)
        cc = slice(j * TN_IN, (j + 1) * TN_IN)
        u = _dot(h_scr[...], wm_ref[:, cols])
        halo = _dot(h_halo, wm_ref[:, cols]) * inside
        xe = jnp.concatenate([halo[:SUBLANES], u, halo[SUBLANES:]], axis=0)
        prev = pltpu.roll(xe, 1, 0)[SUBLANES:SUBLANES + tm]
        nxt = pltpu.roll(xe, tm + 2 * SUBLANES - 1, 0)[SUBLANES:SUBLANES + tm]
        cv = cb_ref[:, cc] + prev * cw_ref[0:1, cc] + u * cw_ref[1:2, cc] + nxt * cw_ref[2:3, cc]
        store(cols.start, cv * jax.nn.sigmoid(cv) * q_scale)

    def gates(br, j):
        out_gate, branch_gate = ((PLAIN_GROUPS + 2 * br + t) * GROUP_W + j * TN_IN for t in (0, 1))
        u = _dot(h_scr[...], wm_ref[:, out_gate:out_gate + TN_IN])
        out_act = u * jax.nn.sigmoid(u) if br == 0 else jax.nn.sigmoid(u)
        w = out_act * jax.nn.sigmoid(_dot(h_scr[...], wm_ref[:, branch_gate:branch_gate + TN_IN]))
        dst = (PLAIN_GROUPS + br) * GROUP_W + j * TN_IN
        store(dst, w * nrm_ref[br:br + 1, j * TN_IN:(j + 1) * TN_IN])

    n_j = GROUP_W // TN_IN
    heavy_tasks = [t for j in range(n_j) for t in ((conv, "mqk", j), (gates, 0, j), (gates, 1, j))]
    light_tasks = [(name, j) for j in range(n_j) for name in ("gqk", "gv", "mv")]
    for heavy, light in zip(heavy_tasks, light_tasks):
        heavy[0](*heavy[1:])
        plain(*light)
    sm_ref[0] = _dot(h_scr[...], ws_ref[...])


def _mixer_kernel(gqk_ref, gv_ref, mqk_ref, mv_ref, wa_ref, wb_ref, sm_ref,
                  w2_ref, b2_ref, selrow_ref, bifrow_ref,
                  out_ref,
                  bcum_scr, att_scr, qin_scr, kst_t_scr, dcol_scr, gt_scr, grow_scr, bt_scr, qkt_scr, kt_scr,
                  a_scr, g_scr, mprev_scr, mnew_scr,
                  sml_scr, qw_scr, ks_t_scr, floor_scr, dint_scr, dec_scr, dn_scr, og_scr, hm_scr, s_scr, c_scr, n_scr,
                  sbf_scr, cbf_scr):
    gqk_ref, gv_ref, mqk_ref, mv_ref, wa_ref, wb_ref, out_ref = (
        r.at[0] for r in (gqk_ref, gv_ref, mqk_ref, mv_ref, wa_ref, wb_ref, out_ref))
    seq = gqk_ref.shape[1]
    nch = seq // CHUNK
    ri = lax.broadcasted_iota(jnp.int32, (CHUNK, CHUNK), 0)
    ci = lax.broadcasted_iota(jnp.int32, (CHUNK, CHUNK), 1)
    lower = ci <= ri
    upper = ci >= ri
    eye = ci == ri
    lower_bf = lower.astype(BF16)
    upper_bf = upper.astype(BF16)
    dirs = ((lower, GLA_MID - 1, CHUNK - 1), (upper, GLA_MID, 0))

    def chunk_rows(c):
        return pl.ds(pl.multiple_of(c * CHUNK, CHUNK), CHUNK)

    def to_col(row):
        return jnp.sum(jnp.where(eye, row, 0.0), axis=1, keepdims=True)

    def act_body(j, carry):
        rows = pl.ds(pl.multiple_of(j * GATE_ROWS, GATE_ROWS), GATE_ROWS)
        sm = sm_ref[0, rows, :]
        hm_scr[0, rows, :] = _logsig(_dot(sm.astype(BF16), w2_ref[0]) + b2_ref[0]) * (LOG2E / GLA_TAU)
        hi, lo = _split_hi_lo(sm)
        g_t = _dot_nt(selrow_ref[0], hi) + _dot_nt(selrow_ref[0], lo) + bifrow_ref[0]
        is_forget = lax.broadcasted_iota(jnp.int32, (SUBLANES, GATE_ROWS), 0) % 2 == 1
        g_t = jnp.where(is_forget, _logsig(g_t), g_t) * LOG2E
        for jj in range(GATE_ROWS // CHUNK):
            gt_scr[j * (GATE_ROWS // CHUNK) + jj] = g_t[:, jj * CHUNK:(jj + 1) * CHUNK]
        return carry

    lax.fori_loop(0, seq // GATE_ROWS, act_body, 0, unroll=PAR_UNROLL)

    row8 = lax.broadcasted_iota(jnp.int32, (SUBLANES, CHUNK), 0)

    def gate_chunk(c, slot):
        rows = chunk_rows(c)
        bcum_scr[slot, 0] = _sel_dot(lower_bf, hm_scr[0, rows, :DK])
        bcum_scr[slot, 1] = _sel_dot(upper_bf, hm_scr[0, rows, DK:])
        r = gt_scr[c]
        cum_f = _dot_sel(r, upper_bf)
        cum_b = _dot_sel(r, lower_bf)
        i_f, b_f, i_b, b_b = r[0:1], cum_f[1:2], r[2:3], cum_b[3:4]
        c_f = i_f - b_f
        c_b = i_b - b_b
        g8 = jnp.zeros((SUBLANES, CHUNK), F32)
        for idx, val in enumerate((i_f, b_f, i_b, b_b, c_f, c_b)):
            g8 = jnp.where(row8 == idx, val, g8)
        grow_scr[c] = g8
        for d, b, cc in ((0, b_f, c_f), (1, b_b, c_b)):
            l_idx = dirs[d][2]
            a = b[:, l_idx:l_idx + 1]
            a_scr[d, c] = _tile(a)
            g_scr[d, c] = _tile(a + jnp.max(cc, axis=1, keepdims=True))
            bt_scr[d, rows, :] = jnp.broadcast_to(to_col(b), (CHUNK, LANES))

    def gla_chunk(c, slot):
        rows = chunk_rows(c)
        qk = gqk_ref[0, rows, :].astype(F32)
        q = qk[:, :DK] * QK_SCALE
        k = qk[:, DK:]
        for d in (0, 1):
            mask, r_idx, l_idx = dirs[d]
            b = bcum_scr[slot, d]
            rho = b[r_idx:r_idx + 1, :]
            b_last = b[l_idx:l_idx + 1, :]
            q_mid = (q * jnp.exp2(b - rho)).astype(BF16)
            k_mid = (k * jnp.exp2(rho - b)).astype(BF16)
            att_scr[d, rows, :] = jnp.where(mask, _dot_nt(q_mid, k_mid), 0.0).astype(BF16)
            qin_scr[d, rows, :] = (q * jnp.exp2(b)).astype(BF16)
            kst_t_scr[d, c] = (k * jnp.exp2(b_last - b)).astype(BF16).T
            dcol_scr[d, c] = jnp.broadcast_to(to_col(jnp.exp2(b_last)), (DK, LANES))
        mqk = mqk_ref[0, rows, :]
        qkt_scr[rows, :] = _dot_nt(mqk[:, :DK], mqk[:, DK:])
        kt_scr[c] = mqk[:, DK:].T

    def pipelined(produce, consume, group):
        for j in range(group):
            produce(j, j)

        def body(q, carry):
            for half in (0, 1):
                base = (2 * q + half) * group
                for j in range(group):
                    produce(jnp.minimum(base + group + j, nch - 1), (1 - half) * group + j)
                for j in range(group):
                    consume(base + j, half * group + j)
            return carry

        lax.fori_loop(0, nch // (2 * group), body, 0)

    pipelined(gate_chunk, gla_chunk, GLA_GROUP)

    def m_body(i, carry):
        m_f, m_b = carry
        cb = nch - 1 - i
        mprev_scr[0, i] = m_f
        mprev_scr[1, cb] = m_b
        m_f = jnp.maximum(a_scr[0, i] + m_f, g_scr[0, i])
        m_b = jnp.maximum(a_scr[1, cb] + m_b, g_scr[1, cb])
        mnew_scr[0, i] = m_f
        mnew_scr[1, cb] = m_b
        return m_f, m_b

    m0 = jnp.full((SUBLANES, LANES), -jnp.inf, F32)
    lax.fori_loop(0, nch, m_body, (m0, m0))

    def prep_dir(d, c, slot):
        rows = chunk_rows(c)
        mask = dirs[d][0]
        g8 = grow_scr[c]
        b_t = bt_scr[d, rows, :]
        c_s = g8[4 + d:5 + d, :]
        m_prev = mprev_scr[d, c][0:1, 0:1]
        m_new = mnew_scr[d, c][0:1, 0:1]
        b_last = a_scr[d, c][0:1, 0:1]
        log_d = jnp.where(mask, b_t + c_s, -jnp.inf)
        m_inter = b_t + m_prev
        m_t = jnp.maximum(m_inter, jnp.max(log_d, axis=1, keepdims=True))
        sml = qkt_scr[rows, :] * jnp.exp2(log_d - m_t)
        sml_bf = sml.astype(BF16)
        sml_scr[slot, d] = sml_bf
        dint_scr[slot, d] = _dot(sml_bf, jnp.ones((CHUNK, LANES), BF16))
        qw_scr[slot, d] = (jnp.exp2(m_inter - m_t) * mqk_ref[0, rows, :DK].astype(F32)).astype(BF16)
        floor_scr[slot, d] = jnp.exp2(-m_t)
        w_st = jnp.broadcast_to(jnp.exp2(b_last + c_s - m_new), (SUBLANES, CHUNK))
        ks_t_scr[slot, d] = (kt_scr[c].astype(F32) * w_st[0:1, :]).astype(BF16)
        dn_scr[slot, d] = _dot(w_st.astype(BF16), mqk_ref[0, rows, DK:])
        dec_scr[slot, d] = _tile(jnp.exp2(b_last + m_prev - m_new))

    def scan_dir(d, c, slot):
        rows = chunk_rows(c)
        v = gv_ref[0, rows, :]
        og_scr[d, rows, :] = _dot(jnp.concatenate([att_scr[d, rows, :], qin_scr[d, rows, :]], axis=1),
                                  jnp.concatenate([v, sbf_scr[d]], axis=0))
        dcol = dcol_scr[d, c]
        s_new = s_scr[d] * jnp.concatenate([dcol, dcol], axis=1) + _dot(kst_t_scr[d, c], v)
        s_scr[d] = s_new
        sbf_scr[d] = s_new.astype(BF16)
        mv = mv_ref[0, rows, :]
        nst = n_scr[d]
        qw = qw_scr[slot, d]
        num = _dot(jnp.concatenate([sml_scr[slot, d], qw], axis=1), jnp.concatenate([mv, cbf_scr[d]], axis=0))
        den = dint_scr[slot, d] + jnp.sum(qw.astype(F32) * nst[0:1, :], axis=1, keepdims=True)
        rden = 1.0 / jnp.maximum(jnp.abs(den), floor_scr[slot, d])
        hm_scr[d, rows, :] = num * jnp.concatenate([rden, rden], axis=1)
        decay = dec_scr[slot, d][0:1, 0:1]
        c_new = decay * c_scr[d] + _dot(ks_t_scr[slot, d], mv)
        c_scr[d] = c_new
        cbf_scr[d] = c_new.astype(BF16)
        n_scr[d] = decay * nst + dn_scr[slot, d]

    for state in (s_scr, sbf_scr, c_scr, cbf_scr, n_scr):
        state[...] = jnp.zeros_like(state)

    def prep_step(i, slot):
        prep_dir(0, i, slot)
        prep_dir(1, nch - 1 - i, slot)

    def scan_step(i, slot):
        scan_dir(0, i, slot)
        scan_dir(1, nch - 1 - i, slot)

    pipelined(prep_step, scan_step, SCAN_GROUP)

    def merge_body(c, carry):
        rows = chunk_rows(c)
        og = og_scr[0, rows, :] + og_scr[1, rows, :]
        hm = hm_scr[0, rows, :] + hm_scr[1, rows, :]
        o_a = og * lax.rsqrt(jnp.mean(og * og, axis=-1, keepdims=True) + EPS)
        o_b = hm * lax.rsqrt(jnp.mean(hm * hm, axis=-1, keepdims=True) + EPS)
        merged = o_a * wa_ref[0, rows, :].astype(F32) + o_b * wb_ref[0, rows, :].astype(F32)
        out_ref[0, rows, :] = merged.astype(BF16)
        return carry

    lax.fori_loop(0, nch, merge_body, 0, unroll=PAR_UNROLL)


def _resident(shape):
    nd = len(shape)
    return pl.BlockSpec(shape, lambda *_: (0,) * nd, pipeline_mode=pl.Buffered(1))


def _params(n_axes):
    return pltpu.CompilerParams(dimension_semantics=("arbitrary",) * n_axes, vmem_limit_bytes=VMEM_LIMIT)


def _ffn1(x2d, g, win, wout):
    m = x2d.shape[0]
    tile = pl.BlockSpec((TM_FFN, D_MODEL), lambda i: (i, 0))
    return pl.pallas_call(
        _ffn1_kernel,
        grid=(m // TM_FFN,),
        in_specs=[tile, _resident(g.shape), _resident(win.shape), _resident(wout.shape)],
        out_specs=tile,
        out_shape=jax.ShapeDtypeStruct((m, D_MODEL), F32),
        scratch_shapes=[pltpu.VMEM((TM_FFN, D_MODEL), BF16), pltpu.VMEM((TM_FFN, D_FF), BF16)],
        compiler_params=_params(1),
        name="ffn1",
    )(x2d, g, win, wout)


def _outffn(x2d, merged, wo, g, win, wout, gf, final_norm):
    m = x2d.shape[0]
    per_seq = merged.shape[2] // TM_FFN
    tile = pl.BlockSpec((TM_FFN, D_MODEL), lambda i: (i, 0))
    heads = pl.BlockSpec((1, N_HEADS, TM_FFN, DV), lambda i: (i // per_seq, 0, i % per_seq, 0))
    return pl.pallas_call(
        functools.partial(_outffn_kernel, final_norm=final_norm),
        grid=(m // TM_FFN,),
        in_specs=[tile, heads, _resident(wo.shape), _resident(g.shape), _resident(win.shape), _resident(wout.shape),
                  _resident(gf.shape)],
        out_specs=tile,
        out_shape=jax.ShapeDtypeStruct((m, D_MODEL), F32),
        scratch_shapes=[pltpu.VMEM((TM_FFN, D_MODEL), BF16), pltpu.VMEM((TM_FFN, D_FF), BF16)],
        compiler_params=_params(1),
        name="outffn",
    )(x2d, merged, wo, g, win, wout, gf)


def _inproj(x, g, wm, ws, nrm, cw, cb):
    bsz, seq, _ = x.shape
    tm = TM_FFN
    per_tile = tm // SUBLANES
    last = seq // SUBLANES - 1
    return pl.pallas_call(
        _inproj_kernel,
        grid=(bsz, seq // tm),
        in_specs=[pl.BlockSpec((1, tm, D_MODEL), lambda b, i: (b, i, 0)),
                  pl.BlockSpec((1, SUBLANES, D_MODEL), lambda b, i: (b, jnp.maximum(i * per_tile - 1, 0), 0)),
                  pl.BlockSpec((1, SUBLANES, D_MODEL), lambda b, i: (b, jnp.minimum((i + 1) * per_tile, last), 0)),
                  _resident(g.shape), _resident(wm.shape), _resident(ws.shape), _resident(nrm.shape),
                  _resident(cw.shape), _resident(cb.shape)],
        out_specs=[pl.BlockSpec((1, N_MAIN // DV, tm, DV), lambda b, i: (b, 0, i, 0)),
                   pl.BlockSpec((1, tm, LANES), lambda b, i: (b, i, 0))],
        out_shape=[jax.ShapeDtypeStruct((bsz, N_MAIN // DV, seq, DV), BF16),
                   jax.ShapeDtypeStruct((bsz, seq, LANES), F32)],
        scratch_shapes=[pltpu.VMEM((tm, D_MODEL), BF16)],
        compiler_params=_params(2),
        name="inproj",
    )(x, x, x, g, wm, ws, nrm, cw, cb)


def _mixer(um, sm, mp):
    bsz, _, seq, _ = um.shape
    nch = seq // CHUNK

    def group(name):
        base = GROUPS.index(name) * N_HEADS
        return pl.BlockSpec((1, 1, seq, DV), lambda b, h: (b, base + h, 0, 0))

    def head(arr):
        nd = arr.ndim
        return pl.BlockSpec((1,) + arr.shape[1:], lambda b, h: (h,) + (0,) * (nd - 1))

    plist = [mp[n] for n in ("w2", "b2", "selrow", "bifrow")]
    in_specs = [group(n) for n in GROUPS] + [pl.BlockSpec((1, seq, LANES), lambda b, h: (b, 0, 0))] \
        + [head(p) for p in plist]
    tok_bf = pltpu.VMEM((2, seq, DK), BF16)
    chunk_t_bf = pltpu.VMEM((2, nch, DK, CHUNK), BF16)
    chunk_tile = pltpu.VMEM((2, nch, SUBLANES, LANES), F32)
    slot_bf = pltpu.VMEM((PIPE_SLOTS, 2, CHUNK, CHUNK), BF16)
    slot_f32 = pltpu.VMEM((PIPE_SLOTS, 2, CHUNK, LANES), F32)
    slot_tile = pltpu.VMEM((PIPE_SLOTS, 2, SUBLANES, LANES), F32)
    return pl.pallas_call(
        _mixer_kernel,
        grid=(bsz, N_HEADS),
        in_specs=in_specs,
        out_specs=pl.BlockSpec((1, 1, seq, DV), lambda b, h: (b, h, 0, 0)),
        out_shape=jax.ShapeDtypeStruct((bsz, N_HEADS, seq, DV), BF16),
        scratch_shapes=[slot_f32,
                        tok_bf, tok_bf, chunk_t_bf,
                        pltpu.VMEM((2, nch, DK, LANES), F32),
                        pltpu.VMEM((nch, SUBLANES, CHUNK), F32),
                        pltpu.VMEM((nch, SUBLANES, CHUNK), F32),
                        pltpu.VMEM((2, seq, LANES), F32),
                        pltpu.VMEM((seq, CHUNK), F32),
                        pltpu.VMEM((nch, DK, CHUNK), BF16),
                        chunk_tile, chunk_tile, chunk_tile, chunk_tile,
                        slot_bf, slot_bf, slot_bf,
                        slot_f32, slot_f32,
                        slot_tile, slot_tile,
                        pltpu.VMEM((2, seq, DV), F32), pltpu.VMEM((2, seq, DV), F32),
                        pltpu.VMEM((2, DK, DV), F32), pltpu.VMEM((2, DK, DV), F32),
                        pltpu.VMEM((2, SUBLANES, DK), F32),
                        pltpu.VMEM((2, DK, DV), BF16), pltpu.VMEM((2, DK, DV), BF16)],
        compiler_params=_params(2),
        name="mixer",
    )(*([um] * len(GROUPS)), sm, *plist)


def _prep_layer(w_in, gla_w2_fwd, gla_b2_fwd, gla_w2_bwd, gla_b2_bwd, gla_norm, ml_conv_w, ml_conv_b, ml_b_if,
                ml_norm):
    sizes = (512, 512, 1024, 1024, 32, 1024, 1024, 1024, 16, 1024, 1024)
    offs = [0]
    for s in sizes:
        offs.append(offs[-1] + s)
    gq, gk, gv, gr, glr, mqk, mv, mo, mif, gate_a, gate_b = [w_in[:, offs[i]:offs[i + 1]] for i in range(len(sizes))]

    def pair(a, b):
        hd = lambda w: w.reshape(D_MODEL, N_HEADS, DK)
        return jnp.concatenate([hd(a), hd(b)], axis=2).reshape(D_MODEL, N_HEADS * 2 * DK)

    parts = dict(gqk=pair(gq, gk), gv=gv, gr=gr, mqk=pair(mqk[:, :512], mqk[:, 512:]), mv=mv, mo=mo, ga=gate_a,
                 gb=gate_b)
    wm = jnp.concatenate([parts[n] for n in GROUPS[:PLAIN_GROUPS] + W_GATES], axis=1).astype(BF16)
    nrm = jnp.stack([gla_norm, ml_norm])
    ws = jnp.concatenate([glr, mif, jnp.zeros((D_MODEL, LANES - 48), F32)], axis=1).astype(BF16)

    hd = lambda a: a.reshape(a.shape[:-1] + (N_HEADS, DK))

    def w2pad(w2, row0):
        w = jnp.moveaxis(hd(w2), 1, 0)
        return jnp.pad(w, ((0, 0), (row0, LANES - row0 - GLA_LOWRANK), (0, 0)))

    w2 = jnp.concatenate([w2pad(gla_w2_fwd, SM_LR_F), w2pad(gla_w2_bwd, SM_LR_B)], axis=2).astype(BF16)
    b2 = jnp.concatenate([hd(gla_b2_fwd), hd(gla_b2_bwd)], axis=1)[:, None, :]

    g_idx = jnp.arange(4)
    h_idx = jnp.arange(N_HEADS)
    src = SM_IF + 4 * g_idx[None, :] + h_idx[:, None]
    selrow = jnp.zeros((N_HEADS, SUBLANES, LANES), F32).at[h_idx[:, None], g_idx[None, :], src].set(1.0)
    bif = jnp.pad(ml_b_if.reshape(4, N_HEADS).T, ((0, 0), (0, SUBLANES - 4)))
    bifrow = jnp.broadcast_to(bif[:, :, None], (N_HEADS, SUBLANES, GATE_ROWS))

    cw = jnp.concatenate([hd(ml_conv_w[:, :512]), hd(ml_conv_w[:, 512:])], axis=2).reshape(3, GROUP_W)
    cb = jnp.concatenate([hd(ml_conv_b[:512]), hd(ml_conv_b[512:])], axis=1).reshape(1, GROUP_W)
    mp = dict(w2=w2, b2=b2, selrow=selrow.astype(BF16), bifrow=bifrow)
    return wm, ws, nrm, cw, cb, mp


def kernel(x_prompt, x_sample, g_ffn1, w_ffn1_in, w_ffn1_out, g_mix, w_in, gla_w2_fwd, gla_b2_fwd, gla_w2_bwd,
           gla_b2_bwd, gla_norm, ml_conv_w, ml_conv_b, ml_b_if, ml_norm, w_out, g_ffn2, w_ffn2_in, w_ffn2_out,
           g_final):
    depth = w_in.shape[0]
    layers = []
    for l in range(depth):
        layers.append(dict(
            g1=g_ffn1[l][None, :], w1i=w_ffn1_in[l].astype(BF16), w1o=w_ffn1_out[l].astype(BF16),
            gm=g_mix[l][None, :],
            mix=_prep_layer(w_in[l], gla_w2_fwd[l], gla_b2_fwd[l], gla_w2_bwd[l], gla_b2_bwd[l], gla_norm[l],
                            ml_conv_w[l], ml_conv_b[l], ml_b_if[l], ml_norm[l]),
            wo=w_out[l].astype(BF16), g2=g_ffn2[l][None, :], w2i=w_ffn2_in[l].astype(BF16),
            w2o=w_ffn2_out[l].astype(BF16)))
    gf = g_final[None, :]

    def trunk(x):
        bsz, seq, _ = x.shape
        for l, p in enumerate(layers):
            wm, ws, nrm, cw, cb, mp = p["mix"]
            x1 = _ffn1(x.reshape(bsz * seq, D_MODEL), p["g1"], p["w1i"], p["w1o"])
            um, sm = _inproj(x1.reshape(bsz, seq, D_MODEL), p["gm"], wm, ws, nrm, cw, cb)
            merged = _mixer(um, sm, mp)
            x = _outffn(x1, merged, p["wo"], p["g2"], p["w2i"], p["w2o"], gf,
                        final_norm=(l == depth - 1))
            x = x.reshape(bsz, seq, D_MODEL)
        return x

    return trunk(x_prompt), trunk(x_sample)
```

```python
import functools

import jax
import jax.numpy as jnp
from jax import lax
from jax.experimental import pallas as pl
from jax.experimental.pallas import tpu as pltpu

F32 = jnp.float32
BF16 = jnp.bfloat16

D_MODEL = 1024
D_FF = 2816
N_HEADS = 4
DK = 128
DV = 256
GLA_LOWRANK = 16
GLA_TAU = 16.0
EPS = 1e-6
QK_SCALE = DK ** -0.5
LOG2E = 1.4426950408889634

LANES = 128
SUBLANES = 8
CHUNK = 128
GLA_MID = CHUNK // 2
TM_FFN = 1024
TM_IN = 512
TF = 256
TN_IN = 256
PAR_UNROLL = 8
GATE_ROWS = 512
GLA_GROUP = 4
SCAN_GROUP = 2
PIPE_SLOTS = 2 * max(GLA_GROUP, SCAN_GROUP)
VMEM_LIMIT = 56 * 1024 * 1024

GROUP_W = N_HEADS * DV
GROUPS = ("gqk", "gv", "mqk", "mv", "wa", "wb")
PLAIN_GROUPS = 4
W_GATES = ("gr", "ga", "mo", "gb")
N_MAIN = len(GROUPS) * GROUP_W
SM_LR_F, SM_LR_B, SM_IF = 0, GLA_LOWRANK, 2 * GLA_LOWRANK


def _dot(a, b):
    return jnp.dot(a, b, preferred_element_type=F32)


def _dot_nt(a, b):
    return lax.dot_general(a, b, (((1,), (1,)), ((), ())), preferred_element_type=F32)


def _rms(x, g):
    return x * lax.rsqrt(jnp.mean(x * x, axis=-1, keepdims=True) + EPS) * g


def _logsig(z):
    return jnp.minimum(z, 0.0) - jnp.log(1.0 + jnp.exp(-jnp.abs(z)))


def _split_hi_lo(x):
    hi = x.astype(BF16)
    lo = (x - hi.astype(F32)).astype(BF16)
    return hi, lo


def _sel_dot(sel_bf16, x):
    hi, lo = _split_hi_lo(x)
    return _dot(sel_bf16, hi) + _dot(sel_bf16, lo)


def _dot_sel(x, sel_bf16):
    hi, lo = _split_hi_lo(x)
    return _dot(hi, sel_bf16) + _dot(lo, sel_bf16)


def _tile(x):
    return jnp.broadcast_to(x, (SUBLANES, LANES))


def _swiglu_into(h_scr, win_ref, wout_ref, act_scr):
    for j in range(D_FF // TF):
        a = _dot(h_scr[...], win_ref[:, j * TF:(j + 1) * TF])
        g = _dot(h_scr[...], win_ref[:, D_FF + j * TF:D_FF + (j + 1) * TF])
        act_scr[:, j * TF:(j + 1) * TF] = (a * jax.nn.sigmoid(a) * g).astype(BF16)
    return _dot(act_scr[...], wout_ref[...])


def _ffn1_kernel(x_ref, g_ref, win_ref, wout_ref, o_ref, h_scr, act_scr):
    x = x_ref[...]
    h_scr[...] = _rms(x, g_ref[...]).astype(BF16)
    o_ref[...] = x + 0.5 * _swiglu_into(h_scr, win_ref, wout_ref, act_scr)


def _outffn_kernel(x_ref, m_ref, wo_ref, g_ref, win_ref, wout_ref, gf_ref, o_ref, h_scr, act_scr, *, final_norm):
    x = x_ref[...]
    for hd in range(N_HEADS):
        x = x + _dot(m_ref[0, hd], wo_ref[hd * DV:(hd + 1) * DV, :])
    h_scr[...] = _rms(x, g_ref[...]).astype(BF16)
    x = x + 0.5 * _swiglu_into(h_scr, win_ref, wout_ref, act_scr)
    o_ref[...] = _rms(x, gf_ref[...]) if final_norm else x


def _inproj_kernel(x_ref, xp_ref, xn_ref, g_ref, wm_ref, ws_ref, nrm_ref, cw_ref, cb_ref, um_ref, sm_ref, h_scr):
    tm = x_ref.shape[1]
    h_scr[...] = _rms(x_ref[0], g_ref[...]).astype(BF16)
    i = pl.program_id(1)
    r16 = lax.broadcasted_iota(jnp.int32, (2 * SUBLANES, 1), 0)
    inside = jnp.where(r16 < SUBLANES, jnp.where(i > 0, 1.0, 0.0), jnp.where(i < pl.num_programs(1) - 1, 1.0, 0.0))
    h_halo = _rms(jnp.concatenate([xp_ref[0], xn_ref[0]], axis=0), g_ref[...]).astype(BF16)
    lane_q = lax.broadcasted_iota(jnp.int32, (1, TN_IN), 1) % (2 * DK) < DK
    q_scale = jnp.where(lane_q, QK_SCALE, 1.0)
    def store(col0, val):
        for k in range(TN_IN // DV):
            um_ref[0, col0 // DV + k] = val[:, k * DV:(k + 1) * DV].astype(BF16)

    def plain(name, j):
        cols = slice(GROUPS.index(name) * GROUP_W + j * TN_IN, GROUPS.index(name) * GROUP_W + (j + 1) * TN_IN)
        store(cols.start, _dot(h_scr[...], wm_ref[:, cols]))

    def conv(name, j):
        cols = slice(GROUPS.index(name) * GROUP_W + j * TN_IN, GROUPS.index(name) * GROUP_W + (j + 1) * TN_IN)
        cc = slice(j * TN_IN, (j + 1) * TN_IN)
        u = _dot(h_scr[...], wm_ref[:, cols])
        halo = _dot(h_halo, wm_ref[:, cols]) * inside
        xe = jnp.concatenate([halo[:SUBLANES], u, halo[SUBLANES:]], axis=0)
        prev = pltpu.roll(xe, 1, 0)[SUBLANES:SUBLANES + tm]
        nxt = pltpu.roll(xe, tm + 2 * SUBLANES - 1, 0)[SUBLANES:SUBLANES + tm]
        cv = cb_ref[:, cc] + prev * cw_ref[0:1, cc] + u * cw_ref[1:2, cc] + nxt * cw_ref[2:3, cc]
        store(cols.start, cv * jax.nn.sigmoid(cv) * q_scale)

    def gates(br, j):
        out_gate, branch_gate = ((PLAIN_GROUPS + 2 * br + t) * GROUP_W + j * TN_IN for t in (0, 1))
        u = _dot(h_scr[...], wm_ref[:, out_gate:out_gate + TN_IN])
        out_act = u * jax.nn.sigmoid(u) if br == 0 else jax.nn.sigmoid(u)
        w = out_act * jax.nn.sigmoid(_dot(h_scr[...], wm_ref[:, branch_gate:branch_gate + TN_IN]))
        dst = (PLAIN_GROUPS + br) * GROUP_W + j * TN_IN
        store(dst, w * nrm_ref[br:br + 1, j * TN_IN:(j + 1) * TN_IN])

    n_j = GROUP_W // TN_IN
    heavy_tasks = [t for j in range(n_j) for t in ((conv, "mqk", j), (gates, 0, j), (gates, 1, j))]
    light_tasks = [(name, j) for j in range(n_j) for name in ("gqk", "gv", "mv")]
    for heavy, light in zip(heavy_tasks, light_tasks):
        heavy[0](*heavy[1:])
        plain(*light)
    sm_ref[0] = _dot(h_scr[...], ws_ref[...])


def _mixer_kernel(gqk_ref, gv_ref, mqk_ref, mv_ref, wa_ref, wb_ref, sm_ref,
                  w2_ref, b2_ref, selrow_ref, bifrow_ref,
                  out_ref,
                  bcum_scr, att_scr, qin_scr, kst_t_scr, dcol_scr, gt_scr, grow_scr, bt_scr, qkt_scr, kt_scr,
                  a_scr, g_scr, mprev_scr, mnew_scr,
                  sml_scr, qw_scr, ks_t_scr, floor_scr, dint_scr, dec_scr, dn_scr, og_scr, hm_scr, s_scr, c_scr, n_scr,
                  sbf_scr, cbf_scr):
    gqk_ref, gv_ref, mqk_ref, mv_ref, wa_ref, wb_ref, out_ref = (
        r.at[0] for r in (gqk_ref, gv_ref, mqk_ref, mv_ref, wa_ref, wb_ref, out_ref))
    seq = gqk_ref.shape[1]
    nch = seq // CHUNK
    ri = lax.broadcasted_iota(jnp.int32, (CHUNK, CHUNK), 0)
    ci = lax.broadcasted_iota(jnp.int32, (CHUNK, CHUNK), 1)
    lower = ci <= ri
    upper = ci >= ri
    eye = ci == ri
    lower_bf = lower.astype(BF16)
    upper_bf = upper.astype(BF16)
    dirs = ((lower, GLA_MID - 1, CHUNK - 1), (upper, GLA_MID, 0))

    def chunk_rows(c):
        return pl.ds(pl.multiple_of(c * CHUNK, CHUNK), CHUNK)

    def to_col(row):
        return jnp.sum(jnp.where(eye, row, 0.0), axis=1, keepdims=True)

    def act_body(j, carry):
        rows = pl.ds(pl.multiple_of(j * GATE_ROWS, GATE_ROWS), GATE_ROWS)
        sm = sm_ref[0, rows, :]
        hm_scr[0, rows, :] = _logsig(_dot(sm.astype(BF16), w2_ref[0]) + b2_ref[0]) * (LOG2E / GLA_TAU)
        hi, lo = _split_hi_lo(sm)
        g_t = _dot_nt(selrow_ref[0], hi) + _dot_nt(selrow_ref[0], lo) + bifrow_ref[0]
        is_forget = lax.broadcasted_iota(jnp.int32, (SUBLANES, GATE_ROWS), 0) % 2 == 1
        g_t = jnp.where(is_forget, _logsig(g_t), g_t) * LOG2E
        for jj in range(GATE_ROWS // CHUNK):
            gt_scr[j * (GATE_ROWS // CHUNK) + jj] = g_t[:, jj * CHUNK:(jj + 1) * CHUNK]
        return carry

    lax.fori_loop(0, seq // GATE_ROWS, act_body, 0, unroll=PAR_UNROLL)

    row8 = lax.broadcasted_iota(jnp.int32, (SUBLANES, CHUNK), 0)

    def gate_chunk(c, slot):
        rows = chunk_rows(c)
        bcum_scr[slot, 0] = _sel_dot(lower_bf, hm_scr[0, rows, :DK])
        bcum_scr[slot, 1] = _sel_dot(upper_bf, hm_scr[0, rows, DK:])
        r = gt_scr[c]
        cum_f = _dot_sel(r, upper_bf)
        cum_b = _dot_sel(r, lower_bf)
        i_f, b_f, i_b, b_b = r[0:1], cum_f[1:2], r[2:3], cum_b[3:4]
        c_f = i_f - b_f
        c_b = i_b - b_b
        g8 = jnp.zeros((SUBLANES, CHUNK), F32)
        for idx, val in enumerate((i_f, b_f, i_b, b_b, c_f, c_b)):
            g8 = jnp.where(row8 == idx, val, g8)
        grow_scr[c] = g8
        for d, b, cc in ((0, b_f, c_f), (1, b_b, c_b)):
            l_idx = dirs[d][2]
            a = b[:, l_idx:l_idx + 1]
            a_scr[d, c] = _tile(a)
            g_scr[d, c] = _tile(a + jnp.max(cc, axis=1, keepdims=True))
            bt_scr[d, rows, :] = jnp.broadcast_to(to_col(b), (CHUNK, LANES))

    def gla_chunk(c, slot):
        rows = chunk_rows(c)
        qk = gqk_ref[0, rows, :].astype(F32)
        q = qk[:, :DK] * QK_SCALE
        k = qk[:, DK:]
        for d in (0, 1):
            mask, r_idx, l_idx = dirs[d]
            b = bcum_scr[slot, d]
            rho = b[r_idx:r_idx + 1, :]
            b_last = b[l_idx:l_idx + 1, :]
            q_mid = (q * jnp.exp2(b - rho)).astype(BF16)
            k_mid = (k * jnp.exp2(rho - b)).astype(BF16)
            att_scr[d, rows, :] = jnp.where(mask, _dot_nt(q_mid, k_mid), 0.0).astype(BF16)
            qin_scr[d, rows, :] = (q * jnp.exp2(b)).astype(BF16)
            kst_t_scr[d, c] = (k * jnp.exp2(b_last - b)).astype(BF16).T
            dcol_scr[d, c] = jnp.broadcast_to(to_col(jnp.exp2(b_last)), (DK, LANES))
        mqk = mqk_ref[0, rows, :]
        qkt_scr[rows, :] = _dot_nt(mqk[:, :DK], mqk[:, DK:])
        kt_scr[c] = mqk[:, DK:].T

    def pipelined(produce, consume, group):
        for j in range(group):
            produce(j, j)

        def body(q, carry):
            for half in (0, 1):
                base = (2 * q + half) * group
                for j in range(group):
                    produce(jnp.minimum(base + group + j, nch - 1), (1 - half) * group + j)
                for j in range(group):
                    consume(base + j, half * group + j)
            return carry

        lax.fori_loop(0, nch // (2 * group), body, 0)

    pipelined(gate_chunk, gla_chunk, GLA_GROUP)

    def m_body(i, carry):
        m_f, m_b = carry
        cb = nch - 1 - i
        mprev_scr[0, i] = m_f
        mprev_scr[1, cb] = m_b
        m_f = jnp.maximum(a_scr[0, i] + m_f, g_scr[0, i])
        m_b = jnp.maximum(a_scr[1, cb] + m_b, g_scr[1, cb])
        mnew_scr[0, i] = m_f
        mnew_scr[1, cb] = m_b
        return m_f, m_b

    m0 = jnp.full((SUBLANES, LANES), -jnp.inf, F32)
    lax.fori_loop(0, nch, m_body, (m0, m0))

    def prep_dir(d, c, slot):
        rows = chunk_rows(c)
        mask = dirs[d][0]
        g8 = grow_scr[c]
        b_t = bt_scr[d, rows, :]
        c_s = g8[4 + d:5 + d, :]
        m_prev = mprev_scr[d, c][0:1, 0:1]
        m_new = mnew_scr[d, c][0:1, 0:1]
        b_last = a_scr[d, c][0:1, 0:1]
        log_d = jnp.where(mask, b_t + c_s, -jnp.inf)
        m_inter = b_t + m_prev
        m_t = jnp.maximum(m_inter, jnp.max(log_d, axis=1, keepdims=True))
        sml = qkt_scr[rows, :] * jnp.exp2(log_d - m_t)
        sml_bf = sml.astype(BF16)
        sml_scr[slot, d] = sml_bf
        dint_scr[slot, d] = _dot(sml_bf, jnp.ones((CHUNK, LANES), BF16))
        qw_scr[slot, d] = (jnp.exp2(m_inter - m_t) * mqk_ref[0, rows, :DK].astype(F32)).astype(BF16)
        floor_scr[slot, d] = jnp.exp2(-m_t)
        w_st = jnp.broadcast_to(jnp.exp2(b_last + c_s - m_new), (SUBLANES, CHUNK))
        ks_t_scr[slot, d] = (kt_scr[c].astype(F32) * w_st[0:1, :]).astype(BF16)
        dn_scr[slot, d] = _dot(w_st.astype(BF16), mqk_ref[0, rows, DK:])
        dec_scr[slot, d] = _tile(jnp.exp2(b_last + m_prev - m_new))

    def scan_dir(d, c, slot):
        rows = chunk_rows(c)
        v = gv_ref[0, rows, :]
        og_scr[d, rows, :] = _dot(jnp.concatenate([att_scr[d, rows, :], qin_scr[d, rows, :]], axis=1),
                                  jnp.concatenate([v, sbf_scr[d]], axis=0))
        dcol = dcol_scr[d, c]
        s_new = s_scr[d] * jnp.concatenate([dcol, dcol], axis=1) + _dot(kst_t_scr[d, c], v)
        s_scr[d] = s_new
        sbf_scr[d] = s_new.astype(BF16)
        mv = mv_ref[0, rows, :]
        nst = n_scr[d]
        qw = qw_scr[slot, d]
        num = _dot(jnp.concatenate([sml_scr[slot, d], qw], axis=1), jnp.concatenate([mv, cbf_scr[d]], axis=0))
        den = dint_scr[slot, d] + jnp.sum(qw.astype(F32) * nst[0:1, :], axis=1, keepdims=True)
        rden = 1.0 / jnp.maximum(jnp.abs(den), floor_scr[slot, d])
        hm_scr[d, rows, :] = num * jnp.concatenate([rden, rden], axis=1)
        decay = dec_scr[slot, d][0:1, 0:1]
        c_new = decay * c_scr[d] + _dot(ks_t_scr[slot, d], mv)
        c_scr[d] = c_new
        cbf_scr[d] = c_new.astype(BF16)
        n_scr[d] = decay * nst + dn_scr[slot, d]

    for state in (s_scr, sbf_scr, c_scr, cbf_scr, n_scr):
        state[...] = jnp.zeros_like(state)

    def prep_step(i, slot):
        prep_dir(0, i, slot)
        prep_dir(1, nch - 1 - i, slot)

    def scan_step(i, slot):
        scan_dir(0, i, slot)
        scan_dir(1, nch - 1 - i, slot)

    pipelined(prep_step, scan_step, SCAN_GROUP)

    def merge_body(c, carry):
        rows = chunk_rows(c)
        og = og_scr[0, rows, :] + og_scr[1, rows, :]
        hm = hm_scr[0, rows, :] + hm_scr[1, rows, :]
        o_a = og * lax.rsqrt(jnp.mean(og * og, axis=-1, keepdims=True) + EPS)
        o_b = hm * lax.rsqrt(jnp.mean(hm * hm, axis=-1, keepdims=True) + EPS)
        merged = o_a * wa_ref[0, rows, :].astype(F32) + o_b * wb_ref[0, rows, :].astype(F32)
        out_ref[0, rows, :] = merged.astype(BF16)
        return carry

    lax.fori_loop(0, nch, merge_body, 0, unroll=PAR_UNROLL)


def _resident(shape):
    nd = len(shape)
    return pl.BlockSpec(shape, lambda *_: (0,) * nd, pipeline_mode=pl.Buffered(1))


def _params(n_axes):
    return pltpu.CompilerParams(dimension_semantics=("arbitrary",) * n_axes, vmem_limit_bytes=VMEM_LIMIT)


def _ffn1(x2d, g, win, wout):
    m = x2d.shape[0]
    tile = pl.BlockSpec((TM_FFN, D_MODEL), lambda i: (i, 0))
    return pl.pallas_call(
        _ffn1_kernel,
        grid=(m // TM_FFN,),
        in_specs=[tile, _resident(g.shape), _resident(win.shape), _resident(wout.shape)],
        out_specs=tile,
        out_shape=jax.ShapeDtypeStruct((m, D_MODEL), F32),
        scratch_shapes=[pltpu.VMEM((TM_FFN, D_MODEL), BF16), pltpu.VMEM((TM_FFN, D_FF), BF16)],
        compiler_params=_params(1),
        name="ffn1",
    )(x2d, g, win, wout)


def _outffn(x2d, merged, wo, g, win, wout, gf, final_norm):
    m = x2d.shape[0]
    per_seq = merged.shape[2] // TM_FFN
    tile = pl.BlockSpec((TM_FFN, D_MODEL), lambda i: (i, 0))
    heads = pl.BlockSpec((1, N_HEADS, TM_FFN, DV), lambda i: (i // per_seq, 0, i % per_seq, 0))
    return pl.pallas_call(
        functools.partial(_outffn_kernel, final_norm=final_norm),
        grid=(m // TM_FFN,),
        in_specs=[tile, heads, _resident(wo.shape), _resident(g.shape), _resident(win.shape), _resident(wout.shape),
                  _resident(gf.shape)],
        out_specs=tile,
        out_shape=jax.ShapeDtypeStruct((m, D_MODEL), F32),
        scratch_shapes=[pltpu.VMEM((TM_FFN, D_MODEL), BF16), pltpu.VMEM((TM_FFN, D_FF), BF16)],
        compiler_params=_params(1),
        name="outffn",
    )(x2d, merged, wo, g, win, wout, gf)


def _inproj(x, g, wm, ws, nrm, cw, cb):
    bsz, seq, _ = x.shape
    tm = TM_IN
    per_tile = tm // SUBLANES
    last = seq // SUBLANES - 1
    return pl.pallas_call(
        _inproj_kernel,
        grid=(bsz, seq // tm),
        in_specs=[pl.BlockSpec((1, tm, D_MODEL), lambda b, i: (b, i, 0)),
                  pl.BlockSpec((1, SUBLANES, D_MODEL), lambda b, i: (b, jnp.maximum(i * per_tile - 1, 0), 0)),
                  pl.BlockSpec((1, SUBLANES, D_MODEL), lambda b, i: (b, jnp.minimum((i + 1) * per_tile, last), 0)),
                  _resident(g.shape), _resident(wm.shape), _resident(ws.shape), _resident(nrm.shape),
                  _resident(cw.shape), _resident(cb.shape)],
        out_specs=[pl.BlockSpec((1, N_MAIN // DV, tm, DV), lambda b, i: (b, 0, i, 0)),
                   pl.BlockSpec((1, tm, LANES), lambda b, i: (b, i, 0))],
        out_shape=[jax.ShapeDtypeStruct((bsz, N_MAIN // DV, seq, DV), BF16),
                   jax.ShapeDtypeStruct((bsz, seq, LANES), F32)],
        scratch_shapes=[pltpu.VMEM((tm, D_MODEL), BF16)],
        compiler_params=_params(2),
        name="inproj",
    )(x, x, x, g, wm, ws, nrm, cw, cb)


def _mixer(um, sm, mp):
    bsz, _, seq, _ = um.shape
    nch = seq // CHUNK

    def group(name):
        base = GROUPS.index(name) * N_HEADS
        return pl.BlockSpec((1, 1, seq, DV), lambda b, h: (b, base + h, 0, 0))

    def head(arr):
        nd = arr.ndim
        return pl.BlockSpec((1,) + arr.shape[1:], lambda b, h: (h,) + (0,) * (nd - 1))

    plist = [mp[n] for n in ("w2", "b2", "selrow", "bifrow")]
    in_specs = [group(n) for n in GROUPS] + [pl.BlockSpec((1, seq, LANES), lambda b, h: (b, 0, 0))] \
        + [head(p) for p in plist]
    tok_bf = pltpu.VMEM((2, seq, DK), BF16)
    chunk_t_bf = pltpu.VMEM((2, nch, DK, CHUNK), BF16)
    chunk_tile = pltpu.VMEM((2, nch, SUBLANES, LANES), F32)
    slot_bf = pltpu.VMEM((PIPE_SLOTS, 2, CHUNK, CHUNK), BF16)
    slot_f32 = pltpu.VMEM((PIPE_SLOTS, 2, CHUNK, LANES), F32)
    slot_tile = pltpu.VMEM((PIPE_SLOTS, 2, SUBLANES, LANES), F32)
    return pl.pallas_call(
        _mixer_kernel,
        grid=(bsz, N_HEADS),
        in_specs=in_specs,
        out_specs=pl.BlockSpec((1, 1, seq, DV), lambda b, h: (b, h, 0, 0)),
        out_shape=jax.ShapeDtypeStruct((bsz, N_HEADS, seq, DV), BF16),
        scratch_shapes=[slot_f32,
                        tok_bf, tok_bf, chunk_t_bf,
                        pltpu.VMEM((2, nch, DK, LANES), F32),
                        pltpu.VMEM((nch, SUBLANES, CHUNK), F32),
                        pltpu.VMEM((nch, SUBLANES, CHUNK), F32),
                        pltpu.VMEM((2, seq, LANES), F32),
                        pltpu.VMEM((seq, CHUNK), F32),
                        pltpu.VMEM((nch, DK, CHUNK), BF16),
                        chunk_tile, chunk_tile, chunk_tile, chunk_tile,
                        slot_bf, slot_bf, slot_bf,
                        slot_f32, slot_f32,
                        slot_tile, slot_tile,
                        pltpu.VMEM((2, seq, DV), F32), pltpu.VMEM((2, seq, DV), F32),
                        pltpu.VMEM((2, DK, DV), F32), pltpu.VMEM((2, DK, DV), F32),
                        pltpu.VMEM((2, SUBLANES, DK), F32),
                        pltpu.VMEM((2, DK, DV), BF16), pltpu.VMEM((2, DK, DV), BF16)],
        compiler_params=_params(2),
        name="mixer",
    )(*([um] * len(GROUPS)), sm, *plist)


def _prep_layer(w_in, gla_w2_fwd, gla_b2_fwd, gla_w2_bwd, gla_b2_bwd, gla_norm, ml_conv_w, ml_conv_b, ml_b_if,
                ml_norm):
    sizes = (512, 512, 1024, 1024, 32, 1024, 1024, 1024, 16, 1024, 1024)
    offs = [0]
    for s in sizes:
        offs.append(offs[-1] + s)
    gq, gk, gv, gr, glr, mqk, mv, mo, mif, gate_a, gate_b = [w_in[:, offs[i]:offs[i + 1]] for i in range(len(sizes))]

    def pair(a, b):
        hd = lambda w: w.reshape(D_MODEL, N_HEADS, DK)
        return jnp.concatenate([hd(a), hd(b)], axis=2).reshape(D_MODEL, N_HEADS * 2 * DK)

    parts = dict(gqk=pair(gq, gk), gv=gv, gr=gr, mqk=pair(mqk[:, :512], mqk[:, 512:]), mv=mv, mo=mo, ga=gate_a,
                 gb=gate_b)
    wm = jnp.concatenate([parts[n] for n in GROUPS[:PLAIN_GROUPS] + W_GATES], axis=1).astype(BF16)
    nrm = jnp.stack([gla_norm, ml_norm])
    ws = jnp.concatenate([glr, mif, jnp.zeros((D_MODEL, LANES - 48), F32)], axis=1).astype(BF16)

    hd = lambda a: a.reshape(a.shape[:-1] + (N_HEADS, DK))

    def w2pad(w2, row0):
        w = jnp.moveaxis(hd(w2), 1, 0)
        return jnp.pad(w, ((0, 0), (row0, LANES - row0 - GLA_LOWRANK), (0, 0)))

    w2 = jnp.concatenate([w2pad(gla_w2_fwd, SM_LR_F), w2pad(gla_w2_bwd, SM_LR_B)], axis=2).astype(BF16)
    b2 = jnp.concatenate([hd(gla_b2_fwd), hd(gla_b2_bwd)], axis=1)[:, None, :]

    g_idx = jnp.arange(4)
    h_idx = jnp.arange(N_HEADS)
    src = SM_IF + 4 * g_idx[None, :] + h_idx[:, None]
    selrow = jnp.zeros((N_HEADS, SUBLANES, LANES), F32).at[h_idx[:, None], g_idx[None, :], src].set(1.0)
    bif = jnp.pad(ml_b_if.reshape(4, N_HEADS).T, ((0, 0), (0, SUBLANES - 4)))
    bifrow = jnp.broadcast_to(bif[:, :, None], (N_HEADS, SUBLANES, GATE_ROWS))

    cw = jnp.concatenate([hd(ml_conv_w[:, :512]), hd(ml_conv_w[:, 512:])], axis=2).reshape(3, GROUP_W)
    cb = jnp.concatenate([hd(ml_conv_b[:512]), hd(ml_conv_b[512:])], axis=1).reshape(1, GROUP_W)
    mp = dict(w2=w2, b2=b2, selrow=selrow.astype(BF16), bifrow=bifrow)
    return wm, ws, nrm, cw, cb, mp


def kernel(x_prompt, x_sample, g_ffn1, w_ffn1_in, w_ffn1_out, g_mix, w_in, gla_w2_fwd, gla_b2_fwd, gla_w2_bwd,
           gla_b2_bwd, gla_norm, ml_conv_w, ml_conv_b, ml_b_if, ml_norm, w_out, g_ffn2, w_ffn2_in, w_ffn2_out,
           g_final):
    depth = w_in.shape[0]
    layers = []
    for l in range(depth):
        layers.append(dict(
            g1=g_ffn1[l][None, :], w1i=w_ffn1_in[l].astype(BF16), w1o=w_ffn1_out[l].astype(BF16),
            gm=g_mix[l][None, :],
            mix=_prep_layer(w_in[l], gla_w2_fwd[l], gla_b2_fwd[l], gla_w2_bwd[l], gla_b2_bwd[l], gla_norm[l],
                            ml_conv_w[l], ml_conv_b[l], ml_b_if[l], ml_norm[l]),
            wo=w_out[l].astype(BF16), g2=g_ffn2[l][None, :], w2i=w_ffn2_in[l].astype(BF16),
            w2o=w_ffn2_out[l].astype(BF16)))
    gf = g_final[None, :]

    def trunk(x):
        bsz, seq, _ = x.shape
        for l, p in enumerate(layers):
            wm, ws, nrm, cw, cb, mp = p["mix"]
            x1 = _ffn1(x.reshape(bsz * seq, D_MODEL), p["g1"], p["w1i"], p["w1o"])
            um, sm = _inproj(x1.reshape(bsz, seq, D_MODEL), p["gm"], wm, ws, nrm, cw, cb)
            merged = _mixer(um, sm, mp)
            x = _outffn(x1, merged, p["wo"], p["g2"], p["w2i"], p["w2o"], gf,
                        final_norm=(l == depth - 1))
            x = x.reshape(bsz, seq, D_MODEL)
        return x

    return trunk(x_prompt), trunk(x_sample)
```

```python
import functools

import jax
import jax.numpy as jnp
from jax import lax
from jax.experimental import pallas as pl
from jax.experimental.pallas import tpu as pltpu

F32 = jnp.float32
BF16 = jnp.bfloat16

D_MODEL = 1024
D_FF = 2816
N_HEADS = 4
DK = 128
DV = 256
GLA_LOWRANK = 16
GLA_TAU = 16.0
EPS = 1e-6
QK_SCALE = DK ** -0.5
LOG2E = 1.4426950408889634

LANES = 128
SUBLANES = 8
CHUNK = 128
GLA_MID = CHUNK // 2
TM_FFN = 1024
TM_IN = 1024
TF = 256
TN_IN = 256
PAR_UNROLL = 8
GATE_ROWS = 512
GLA_GROUP = 4
SCAN_GROUP = 2
PIPE_SLOTS = 2 * max(GLA_GROUP, SCAN_GROUP)
VMEM_LIMIT = 56 * 1024 * 1024

GROUP_W = N_HEADS * DV
GROUPS = ("gqk", "gv", "mqk", "mv", "wa", "wb")
PLAIN_GROUPS = 4
W_GATES = ("gr", "ga", "mo", "gb")
N_MAIN = len(GROUPS) * GROUP_W
SM_LR_F, SM_LR_B, SM_IF = 0, GLA_LOWRANK, 2 * GLA_LOWRANK


def _dot(a, b):
    return jnp.dot(a, b, preferred_element_type=F32)


def _dot_nt(a, b):
    return lax.dot_general(a, b, (((1,), (1,)), ((), ())), preferred_element_type=F32)


def _rms(x, g):
    return x * lax.rsqrt(jnp.mean(x * x, axis=-1, keepdims=True) + EPS) * g


def _logsig(z):
    return jnp.minimum(z, 0.0) - jnp.log(1.0 + jnp.exp(-jnp.abs(z)))


def _split_hi_lo(x):
    hi = x.astype(BF16)
    lo = (x - hi.astype(F32)).astype(BF16)
    return hi, lo


def _sel_dot(sel_bf16, x):
    hi, lo = _split_hi_lo(x)
    return _dot(sel_bf16, hi) + _dot(sel_bf16, lo)


def _dot_sel(x, sel_bf16):
    hi, lo = _split_hi_lo(x)
    return _dot(hi, sel_bf16) + _dot(lo, sel_bf16)


def _tile(x):
    return jnp.broadcast_to(x, (SUBLANES, LANES))


def _swiglu_into(h_scr, win_ref, wout_ref, act_scr):
    for j in range(D_FF // TF):
        a = _dot(h_scr[...], win_ref[:, j * TF:(j + 1) * TF])
        g = _dot(h_scr[...], win_ref[:, D_FF + j * TF:D_FF + (j + 1) * TF])
        act_scr[:, j * TF:(j + 1) * TF] = (a * jax.nn.sigmoid(a) * g).astype(BF16)
    return _dot(act_scr[...], wout_ref[...])


def _ffn1_kernel(x_ref, g_ref, win_ref, wout_ref, o_ref, h_scr, act_scr):
    x = x_ref[...]
    h_scr[...] = _rms(x, g_ref[...]).astype(BF16)
    o_ref[...] = x + 0.5 * _swiglu_into(h_scr, win_ref, wout_ref, act_scr)


def _outffn_kernel(x_ref, m_ref, wo_ref, g_ref, win_ref, wout_ref, gf_ref, o_ref, h_scr, act_scr, *, final_norm):
    x = x_ref[...]
    for hd in range(N_HEADS):
        x = x + _dot(m_ref[0, hd], wo_ref[hd * DV:(hd + 1) * DV, :])
    h_scr[...] = _rms(x, g_ref[...]).astype(BF16)
    x = x + 0.5 * _swiglu_into(h_scr, win_ref, wout_ref, act_scr)
    o_ref[...] = _rms(x, gf_ref[...]) if final_norm else x


def _inproj_kernel(x_ref, xp_ref, xn_ref, g_ref, wm_ref, ws_ref, nrm_ref, cw_ref, cb_ref, um_ref, sm_ref, h_scr):
    tm = x_ref.shape[1]
    h_scr[...] = _rms(x_ref[0], g_ref[...]).astype(BF16)
    i = pl.program_id(1)
    r16 = lax.broadcasted_iota(jnp.int32, (2 * SUBLANES, 1), 0)
    inside = jnp.where(r16 < SUBLANES, jnp.where(i > 0, 1.0, 0.0), jnp.where(i < pl.num_programs(1) - 1, 1.0, 0.0))
    h_halo = _rms(jnp.concatenate([xp_ref[0], xn_ref[0]], axis=0), g_ref[...]).astype(BF16)
    lane_q = lax.broadcasted_iota(jnp.int32, (1, TN_IN), 1) % (2 * DK) < DK
    q_scale = jnp.where(lane_q, QK_SCALE, 1.0)
    def store(col0, val):
        for k in range(TN_IN // DV):
            um_ref[0, col0 // DV + k] = val[:, k * DV:(k + 1) * DV].astype(BF16)

    def plain(name, j):
        cols = slice(GROUPS.index(name) * GROUP_W + j * TN_IN, GROUPS.index(name) * GROUP_W + (j + 1) * TN_IN)
        store(cols.start, _dot(h_scr[...], wm_ref[:, cols]))

    def conv(name, j):
        cols = slice(GROUPS.index(name) * GROUP_W + j * TN_IN, GROUPS.index(name) * GROUP_W + (j + 1) * TN_IN)
        cc = slice(j * TN_IN, (j + 1) * TN_IN)
        u = _dot(h_scr[...], wm_ref[:, cols])
        halo = _dot(h_halo, wm_ref[:, cols]) * inside
        xe = jnp.concatenate([halo[:SUBLANES], u, halo[SUBLANES:]], axis=0)
        prev = pltpu.roll(xe, 1, 0)[SUBLANES:SUBLANES + tm]
        nxt = pltpu.roll(xe, tm + 2 * SUBLANES - 1, 0)[SUBLANES:SUBLANES + tm]
        cv = cb_ref[:, cc] + prev * cw_ref[0:1, cc] + u * cw_ref[1:2, cc] + nxt * cw_ref[2:3, cc]
        store(cols.start, cv * jax.nn.sigmoid(cv) * q_scale)

    def gates(br, j):
        out_gate, branch_gate = ((PLAIN_GROUPS + 2 * br + t) * GROUP_W + j * TN_IN for t in (0, 1))
        u = _dot(h_scr[...], wm_ref[:, out_gate:out_gate + TN_IN])
        out_act = u * jax.nn.sigmoid(u) if br == 0 else jax.nn.sigmoid(u)
        w = out_act * jax.nn.sigmoid(_dot(h_scr[...], wm_ref[:, branch_gate:branch_gate + TN_IN]))
        dst = (PLAIN_GROUPS + br) * GROUP_W + j * TN_IN
        store(dst, w * nrm_ref[br:br + 1, j * TN_IN:(j + 1) * TN_IN])

    n_j = GROUP_W // TN_IN
    heavy_tasks = [t for j in range(n_j) for t in ((conv, "mqk", j), (gates, 0, j), (gates, 1, j))]
    light_tasks = [(name, j) for j in range(n_j) for name in ("gqk", "gv", "mv")]
    for heavy, light in zip(heavy_tasks, light_tasks):
        heavy[0](*heavy[1:])
        plain(*light)
    sm_ref[0] = _dot(h_scr[...], ws_ref[...])


def _mixer_kernel(gqk_ref, gv_ref, mqk_ref, mv_ref, wa_ref, wb_ref, sm_ref,
                  w2_ref, b2_ref, selrow_ref, bifrow_ref,
                  out_ref,
                  bcum_scr, att_scr, qin_scr, kst_t_scr, dcol_scr, gt_scr, grow_scr, bt_scr, qkt_scr, kt_scr,
                  a_scr, g_scr, mprev_scr, mnew_scr,
                  sml_scr, qw_scr, ks_t_scr, floor_scr, dint_scr, dec_scr, dn_scr, og_scr, hm_scr, s_scr, c_scr, n_scr,
                  sbf_scr, cbf_scr):
    gqk_ref, gv_ref, mqk_ref, mv_ref, wa_ref, wb_ref, out_ref = (
        r.at[0] for r in (gqk_ref, gv_ref, mqk_ref, mv_ref, wa_ref, wb_ref, out_ref))
    seq = gqk_ref.shape[1]
    nch = seq // CHUNK
    ri = lax.broadcasted_iota(jnp.int32, (CHUNK, CHUNK), 0)
    ci = lax.broadcasted_iota(jnp.int32, (CHUNK, CHUNK), 1)
    lower = ci <= ri
    upper = ci >= ri
    eye = ci == ri
    lower_bf = lower.astype(BF16)
    upper_bf = upper.astype(BF16)
    dirs = ((lower, GLA_MID - 1, CHUNK - 1), (upper, GLA_MID, 0))

    def chunk_rows(c):
        return pl.ds(pl.multiple_of(c * CHUNK, CHUNK), CHUNK)

    def to_col(row):
        return jnp.sum(jnp.where(eye, row, 0.0), axis=1, keepdims=True)

    def act_body(j, carry):
        rows = pl.ds(pl.multiple_of(j * GATE_ROWS, GATE_ROWS), GATE_ROWS)
        sm = sm_ref[0, rows, :]
        hm_scr[0, rows, :] = _logsig(_dot(sm.astype(BF16), w2_ref[0]) + b2_ref[0]) * (LOG2E / GLA_TAU)
        hi, lo = _split_hi_lo(sm)
        g_t = _dot_nt(selrow_ref[0], hi) + _dot_nt(selrow_ref[0], lo) + bifrow_ref[0]
        is_forget = lax.broadcasted_iota(jnp.int32, (SUBLANES, GATE_ROWS), 0) % 2 == 1
        g_t = jnp.where(is_forget, _logsig(g_t), g_t) * LOG2E
        for jj in range(GATE_ROWS // CHUNK):
            gt_scr[j * (GATE_ROWS // CHUNK) + jj] = g_t[:, jj * CHUNK:(jj + 1) * CHUNK]
        return carry

    lax.fori_loop(0, seq // GATE_ROWS, act_body, 0, unroll=PAR_UNROLL)

    row8 = lax.broadcasted_iota(jnp.int32, (SUBLANES, CHUNK), 0)

    def gate_chunk(c, slot):
        rows = chunk_rows(c)
        bcum_scr[slot, 0] = _sel_dot(lower_bf, hm_scr[0, rows, :DK])
        bcum_scr[slot, 1] = _sel_dot(upper_bf, hm_scr[0, rows, DK:])
        r = gt_scr[c]
        cum_f = _dot_sel(r, upper_bf)
        cum_b = _dot_sel(r, lower_bf)
        i_f, b_f, i_b, b_b = r[0:1], cum_f[1:2], r[2:3], cum_b[3:4]
        c_f = i_f - b_f
        c_b = i_b - b_b
        g8 = jnp.zeros((SUBLANES, CHUNK), F32)
        for idx, val in enumerate((i_f, b_f, i_b, b_b, c_f, c_b)):
            g8 = jnp.where(row8 == idx, val, g8)
        grow_scr[c] = g8
        for d, b, cc in ((0, b_f, c_f), (1, b_b, c_b)):
            l_idx = dirs[d][2]
            a = b[:, l_idx:l_idx + 1]
            a_scr[d, c] = _tile(a)
            g_scr[d, c] = _tile(a + jnp.max(cc, axis=1, keepdims=True))
            bt_scr[d, rows, :] = jnp.broadcast_to(to_col(b), (CHUNK, LANES))

    def gla_chunk(c, slot):
        rows = chunk_rows(c)
        qk = gqk_ref[0, rows, :].astype(F32)
        q = qk[:, :DK] * QK_SCALE
        k = qk[:, DK:]
        for d in (0, 1):
            mask, r_idx, l_idx = dirs[d]
            b = bcum_scr[slot, d]
            rho = b[r_idx:r_idx + 1, :]
            b_last = b[l_idx:l_idx + 1, :]
            q_mid = (q * jnp.exp2(b - rho)).astype(BF16)
            k_mid = (k * jnp.exp2(rho - b)).astype(BF16)
            att_scr[d, rows, :] = jnp.where(mask, _dot_nt(q_mid, k_mid), 0.0).astype(BF16)
            qin_scr[d, rows, :] = (q * jnp.exp2(b)).astype(BF16)
            kst_t_scr[d, c] = (k * jnp.exp2(b_last - b)).astype(BF16).T
            dcol_scr[d, c] = jnp.broadcast_to(to_col(jnp.exp2(b_last)), (DK, LANES))
        mqk = mqk_ref[0, rows, :]
        qkt_scr[rows, :] = _dot_nt(mqk[:, :DK], mqk[:, DK:])
        kt_scr[c] = mqk[:, DK:].T

    def pipelined(produce, consume, group):
        for j in range(group):
            produce(j, j)

        def body(q, carry):
            for half in (0, 1):
                base = (2 * q + half) * group
                for j in range(group):
                    produce(jnp.minimum(base + group + j, nch - 1), (1 - half) * group + j)
                for j in range(group):
                    consume(base + j, half * group + j)
            return carry

        lax.fori_loop(0, nch // (2 * group), body, 0)

    pipelined(gate_chunk, gla_chunk, GLA_GROUP)

    def m_body(i, carry):
        m_f, m_b = carry
        cb = nch - 1 - i
        mprev_scr[0, i] = m_f
        mprev_scr[1, cb] = m_b
        m_f = jnp.maximum(a_scr[0, i] + m_f, g_scr[0, i])
        m_b = jnp.maximum(a_scr[1, cb] + m_b, g_scr[1, cb])
        mnew_scr[0, i] = m_f
        mnew_scr[1, cb] = m_b
        return m_f, m_b

    m0 = jnp.full((SUBLANES, LANES), -jnp.inf, F32)
    lax.fori_loop(0, nch, m_body, (m0, m0))

    def prep_dir(d, c, slot):
        rows = chunk_rows(c)
        mask = dirs[d][0]
        g8 = grow_scr[c]
        b_t = bt_scr[d, rows, :]
        c_s = g8[4 + d:5 + d, :]
        m_prev = mprev_scr[d, c][0:1, 0:1]
        m_new = mnew_scr[d, c][0:1, 0:1]
        b_last = a_scr[d, c][0:1, 0:1]
        log_d = jnp.where(mask, b_t + c_s, -jnp.inf)
        m_inter = b_t + m_prev
        m_t = jnp.maximum(m_inter, jnp.max(log_d, axis=1, keepdims=True))
        sml = qkt_scr[rows, :] * jnp.exp2(log_d - m_t)
        sml_bf = sml.astype(BF16)
        sml_scr[slot, d] = sml_bf
        dint_scr[slot, d] = _dot(sml_bf, jnp.ones((CHUNK, LANES), BF16))
        qw_scr[slot, d] = (jnp.exp2(m_inter - m_t) * mqk_ref[0, rows, :DK].astype(F32)).astype(BF16)
        floor_scr[slot, d] = jnp.exp2(-m_t)
        w_st = jnp.broadcast_to(jnp.exp2(b_last + c_s - m_new), (SUBLANES, CHUNK))
        ks_t_scr[slot, d] = (kt_scr[c].astype(F32) * w_st[0:1, :]).astype(BF16)
        dn_scr[slot, d] = _dot(w_st.astype(BF16), mqk_ref[0, rows, DK:])
        dec_scr[slot, d] = _tile(jnp.exp2(b_last + m_prev - m_new))

    def scan_dir(d, c, slot):
        rows = chunk_rows(c)
        v = gv_ref[0, rows, :]
        og_scr[d, rows, :] = _dot(jnp.concatenate([att_scr[d, rows, :], qin_scr[d, rows, :]], axis=1),
                                  jnp.concatenate([v, sbf_scr[d]], axis=0))
        dcol = dcol_scr[d, c]
        s_new = s_scr[d] * jnp.concatenate([dcol, dcol], axis=1) + _dot(kst_t_scr[d, c], v)
        s_scr[d] = s_new
        sbf_scr[d] = s_new.astype(BF16)
        mv = mv_ref[0, rows, :]
        nst = n_scr[d]
        qw = qw_scr[slot, d]
        num = _dot(jnp.concatenate([sml_scr[slot, d], qw], axis=1), jnp.concatenate([mv, cbf_scr[d]], axis=0))
        den = dint_scr[slot, d] + jnp.sum(qw.astype(F32) * nst[0:1, :], axis=1, keepdims=True)
        rden = 1.0 / jnp.maximum(jnp.abs(den), floor_scr[slot, d])
        hm_scr[d, rows, :] = num * jnp.concatenate([rden, rden], axis=1)
        decay = dec_scr[slot, d][0:1, 0:1]
        c_new = decay * c_scr[d] + _dot(ks_t_scr[slot, d], mv)
        c_scr[d] = c_new
        cbf_scr[d] = c_new.astype(BF16)
        n_scr[d] = decay * nst + dn_scr[slot, d]

    for state in (s_scr, sbf_scr, c_scr, cbf_scr, n_scr):
        state[...] = jnp.zeros_like(state)

    def prep_step(i, slot):
        prep_dir(0, i, slot)
        prep_dir(1, nch - 1 - i, slot)

    def scan_step(i, slot):
        scan_dir(0, i, slot)
        scan_dir(1, nch - 1 - i, slot)

    pipelined(prep_step, scan_step, SCAN_GROUP)

    def merge_body(c, carry):
        rows = chunk_rows(c)
        og = og_scr[0, rows, :] + og_scr[1, rows, :]
        hm = hm_scr[0, rows, :] + hm_scr[1, rows, :]
        o_a = og * lax.rsqrt(jnp.mean(og * og, axis=-1, keepdims=True) + EPS)
        o_b = hm * lax.rsqrt(jnp.mean(hm * hm, axis=-1, keepdims=True) + EPS)
        merged = o_a * wa_ref[0, rows, :].astype(F32) + o_b * wb_ref[0, rows, :].astype(F32)
        out_ref[0, rows, :] = merged.astype(BF16)
        return carry

    lax.fori_loop(0, nch, merge_body, 0, unroll=PAR_UNROLL)


def _resident(shape):
    nd = len(shape)
    return pl.BlockSpec(shape, lambda *_: (0,) * nd, pipeline_mode=pl.Buffered(1))


def _params(n_axes):
    return pltpu.CompilerParams(dimension_semantics=("arbitrary",) * n_axes, vmem_limit_bytes=VMEM_LIMIT)


def _ffn1(x2d, g, win, wout):
    m = x2d.shape[0]
    tile = pl.BlockSpec((TM_FFN, D_MODEL), lambda i: (i, 0))
    return pl.pallas_call(
        _ffn1_kernel,
        grid=(m // TM_FFN,),
        in_specs=[tile, _resident(g.shape), _resident(win.shape), _resident(wout.shape)],
        out_specs=tile,
        out_shape=jax.ShapeDtypeStruct((m, D_MODEL), F32),
        scratch_shapes=[pltpu.VMEM((TM_FFN, D_MODEL), BF16), pltpu.VMEM((TM_FFN, D_FF), BF16)],
        compiler_params=_params(1),
        name="ffn1",
    )(x2d, g, win, wout)


def _outffn(x2d, merged, wo, g, win, wout, gf, final_norm):
    m = x2d.shape[0]
    per_seq = merged.shape[2] // TM_FFN
    tile = pl.BlockSpec((TM_FFN, D_MODEL), lambda i: (i, 0))
    heads = pl.BlockSpec((1, N_HEADS, TM_FFN, DV), lambda i: (i // per_seq, 0, i % per_seq, 0))
    return pl.pallas_call(
        functools.partial(_outffn_kernel, final_norm=final_norm),
        grid=(m // TM_FFN,),
        in_specs=[tile, heads, _resident(wo.shape), _resident(g.shape), _resident(win.shape), _resident(wout.shape),
                  _resident(gf.shape)],
        out_specs=tile,
        out_shape=jax.ShapeDtypeStruct((m, D_MODEL), F32),
        scratch_shapes=[pltpu.VMEM((TM_FFN, D_MODEL), BF16), pltpu.VMEM((TM_FFN, D_FF), BF16)],
        compiler_params=_params(1),
        name="outffn",
    )(x2d, merged, wo, g, win, wout, gf)


def _inproj(x, g, wm, ws, nrm, cw, cb):
    bsz, seq, _ = x.shape
    tm = TM_IN
    per_tile = tm // SUBLANES
    last = seq // SUBLANES - 1
    return pl.pallas_call(
        _inproj_kernel,
        grid=(bsz, seq // tm),
        in_specs=[pl.BlockSpec((1, tm, D_MODEL), lambda b, i: (b, i, 0)),
                  pl.BlockSpec((1, SUBLANES, D_MODEL), lambda b, i: (b, jnp.maximum(i * per_tile - 1, 0), 0)),
                  pl.BlockSpec((1, SUBLANES, D_MODEL), lambda b, i: (b, jnp.minimum((i + 1) * per_tile, last), 0)),
                  _resident(g.shape), _resident(wm.shape), _resident(ws.shape), _resident(nrm.shape),
                  _resident(cw.shape), _resident(cb.shape)],
        out_specs=[pl.BlockSpec((1, N_MAIN // DV, tm, DV), lambda b, i: (b, 0, i, 0)),
                   pl.BlockSpec((1, tm, LANES), lambda b, i: (b, i, 0))],
        out_shape=[jax.ShapeDtypeStruct((bsz, N_MAIN // DV, seq, DV), BF16),
                   jax.ShapeDtypeStruct((bsz, seq, LANES), F32)],
        scratch_shapes=[pltpu.VMEM((tm, D_MODEL), BF16)],
        compiler_params=_params(2),
        name="inproj",
    )(x, x, x, g, wm, ws, nrm, cw, cb)


def _mixer(um, sm, mp):
    bsz, _, seq, _ = um.shape
    nch = seq // CHUNK

    def group(name):
        base = GROUPS.index(name) * N_HEADS
        return pl.BlockSpec((1, 1, seq, DV), lambda b, h: (b, base + h, 0, 0))

    def head(arr):
        nd = arr.ndim
        return pl.BlockSpec((1,) + arr.shape[1:], lambda b, h: (h,) + (0,) * (nd - 1))

    plist = [mp[n] for n in ("w2", "b2", "selrow", "bifrow")]
    in_specs = [group(n) for n in GROUPS] + [pl.BlockSpec((1, seq, LANES), lambda b, h: (b, 0, 0))] \
        + [head(p) for p in plist]
    tok_bf = pltpu.VMEM((2, seq, DK), BF16)
    chunk_t_bf = pltpu.VMEM((2, nch, DK, CHUNK), BF16)
    chunk_tile = pltpu.VMEM((2, nch, SUBLANES, LANES), F32)
    slot_bf = pltpu.VMEM((PIPE_SLOTS, 2, CHUNK, CHUNK), BF16)
    slot_f32 = pltpu.VMEM((PIPE_SLOTS, 2, CHUNK, LANES), F32)
    slot_tile = pltpu.VMEM((PIPE_SLOTS, 2, SUBLANES, LANES), F32)
    return pl.pallas_call(
        _mixer_kernel,
        grid=(bsz, N_HEADS),
        in_specs=in_specs,
        out_specs=pl.BlockSpec((1, 1, seq, DV), lambda b, h: (b, h, 0, 0)),
        out_shape=jax.ShapeDtypeStruct((bsz, N_HEADS, seq, DV), BF16),
        scratch_shapes=[slot_f32,
                        tok_bf, tok_bf, chunk_t_bf,
                        pltpu.VMEM((2, nch, DK, LANES), F32),
                        pltpu.VMEM((nch, SUBLANES, CHUNK), F32),
                        pltpu.VMEM((nch, SUBLANES, CHUNK), F32),
                        pltpu.VMEM((2, seq, LANES), F32),
                        pltpu.VMEM((seq, CHUNK), F32),
                        pltpu.VMEM((nch, DK, CHUNK), BF16),
                        chunk_tile, chunk_tile, chunk_tile, chunk_tile,
                        slot_bf, slot_bf, slot_bf,
                        slot_f32, slot_f32,
                        slot_tile, slot_tile,
                        pltpu.VMEM((2, seq, DV), F32), pltpu.VMEM((2, seq, DV), F32),
                        pltpu.VMEM((2, DK, DV), F32), pltpu.VMEM((2, DK, DV), F32),
                        pltpu.VMEM((2, SUBLANES, DK), F32),
                        pltpu.VMEM((2, DK, DV), BF16), pltpu.VMEM((2, DK, DV), BF16)],
        compiler_params=_params(2),
        name="mixer",
    )(*([um] * len(GROUPS)), sm, *plist)


def _prep_layer(w_in, gla_w2_fwd, gla_b2_fwd, gla_w2_bwd, gla_b2_bwd, gla_norm, ml_conv_w, ml_conv_b, ml_b_if,
                ml_norm):
    sizes = (512, 512, 1024, 1024, 32, 1024, 1024, 1024, 16, 1024, 1024)
    offs = [0]
    for s in sizes:
        offs.append(offs[-1] + s)
    gq, gk, gv, gr, glr, mqk, mv, mo, mif, gate_a, gate_b = [w_in[:, offs[i]:offs[i + 1]] for i in range(len(sizes))]

    def pair(a, b):
        hd = lambda w: w.reshape(D_MODEL, N_HEADS, DK)
        return jnp.concatenate([hd(a), hd(b)], axis=2).reshape(D_MODEL, N_HEADS * 2 * DK)

    parts = dict(gqk=pair(gq, gk), gv=gv, gr=gr, mqk=pair(mqk[:, :512], mqk[:, 512:]), mv=mv, mo=mo, ga=gate_a,
                 gb=gate_b)
    wm = jnp.concatenate([parts[n] for n in GROUPS[:PLAIN_GROUPS] + W_GATES], axis=1).astype(BF16)
    nrm = jnp.stack([gla_norm, ml_norm])
    ws = jnp.concatenate([glr, mif, jnp.zeros((D_MODEL, LANES - 48), F32)], axis=1).astype(BF16)

    hd = lambda a: a.reshape(a.shape[:-1] + (N_HEADS, DK))

    def w2pad(w2, row0):
        w = jnp.moveaxis(hd(w2), 1, 0)
        return jnp.pad(w, ((0, 0), (row0, LANES - row0 - GLA_LOWRANK), (0, 0)))

    w2 = jnp.concatenate([w2pad(gla_w2_fwd, SM_LR_F), w2pad(gla_w2_bwd, SM_LR_B)], axis=2).astype(BF16)
    b2 = jnp.concatenate([hd(gla_b2_fwd), hd(gla_b2_bwd)], axis=1)[:, None, :]

    g_idx = jnp.arange(4)
    h_idx = jnp.arange(N_HEADS)
    src = SM_IF + 4 * g_idx[None, :] + h_idx[:, None]
    selrow = jnp.zeros((N_HEADS, SUBLANES, LANES), F32).at[h_idx[:, None], g_idx[None, :], src].set(1.0)
    bif = jnp.pad(ml_b_if.reshape(4, N_HEADS).T, ((0, 0), (0, SUBLANES - 4)))
    bifrow = jnp.broadcast_to(bif[:, :, None], (N_HEADS, SUBLANES, GATE_ROWS))

    cw = jnp.concatenate([hd(ml_conv_w[:, :512]), hd(ml_conv_w[:, 512:])], axis=2).reshape(3, GROUP_W)
    cb = jnp.concatenate([hd(ml_conv_b[:512]), hd(ml_conv_b[512:])], axis=1).reshape(1, GROUP_W)
    mp = dict(w2=w2, b2=b2, selrow=selrow.astype(BF16), bifrow=bifrow)
    return wm, ws, nrm, cw, cb, mp


def kernel(x_prompt, x_sample, g_ffn1, w_ffn1_in, w_ffn1_out, g_mix, w_in, gla_w2_fwd, gla_b2_fwd, gla_w2_bwd,
           gla_b2_bwd, gla_norm, ml_conv_w, ml_conv_b, ml_b_if, ml_norm, w_out, g_ffn2, w_ffn2_in, w_ffn2_out,
           g_final):
    depth = w_in.shape[0]
    layers = []
    for l in range(depth):
        layers.append(dict(
            g1=g_ffn1[l][None, :], w1i=w_ffn1_in[l].astype(BF16), w1o=w_ffn1_out[l].astype(BF16),
            gm=g_mix[l][None, :],
            mix=_prep_layer(w_in[l], gla_w2_fwd[l], gla_b2_fwd[l], gla_w2_bwd[l], gla_b2_bwd[l], gla_norm[l],
                            ml_conv_w[l], ml_conv_b[l], ml_b_if[l], ml_norm[l]),
            wo=w_out[l].astype(BF16), g2=g_ffn2[l][None, :], w2i=w_ffn2_in[l].astype(BF16),
            w2o=w_ffn2_out[l].astype(BF16)))
    gf = g_final[None, :]

    def trunk(x):
        bsz, seq, _ = x.shape
        for l, p in enumerate(layers):
            wm, ws, nrm, cw, cb, mp = p["mix"]
            x1 = _ffn1(x.reshape(bsz * seq, D_MODEL), p["g1"], p["w1i"], p["w1o"])
            um, sm = _inproj(x1.reshape(bsz, seq, D_MODEL), p["gm"], wm, ws, nrm, cw, cb)
            merged = _mixer(um, sm, mp)
            x = _outffn(x1, merged, p["wo"], p["g2"], p["w2i"], p["w2o"], gf,
                        final_norm=(l == depth - 1))
            x = x.reshape(bsz, seq, D_MODEL)
        return x

    return trunk(x_prompt), trunk(x_sample)
```

```python
import functools

import jax
import jax.numpy as jnp
from jax import lax
from jax.experimental import pallas as pl
from jax.experimental.pallas import tpu as pltpu

F32 = jnp.float32
BF16 = jnp.bfloat16

D_MODEL = 1024
D_FF = 2816
N_HEADS = 4
DK = 128
DV = 256
GLA_LOWRANK = 16
GLA_TAU = 16.0
EPS = 1e-6
QK_SCALE = DK ** -0.5
LOG2E = 1.4426950408889634

LANES = 128
SUBLANES = 8
CHUNK = 128
GLA_MID = CHUNK // 2
TM_FFN = 1024
TM_IN = 1024
TF = 256
TN_IN = 256
PAR_UNROLL = 8
GATE_ROWS = 512
GLA_GROUP = 4
SCAN_GROUP = 2
PIPE_SLOTS = 2 * max(GLA_GROUP, SCAN_GROUP)
VMEM_LIMIT = 56 * 1024 * 1024

GROUP_W = N_HEADS * DV
GROUPS = ("gqk", "gv", "mqk", "mv", "wa", "wb")
PLAIN_GROUPS = 4
W_GATES = ("gr", "ga", "mo", "gb")
N_MAIN = len(GROUPS) * GROUP_W
SM_LR_F, SM_LR_B, SM_IF = 0, GLA_LOWRANK, 2 * GLA_LOWRANK


def _dot(a, b):
    return jnp.dot(a, b, preferred_element_type=F32)


def _dot_nt(a, b):
    return lax.dot_general(a, b, (((1,), (1,)), ((), ())), preferred_element_type=F32)


def _rms(x, g):
    return x * lax.rsqrt(jnp.mean(x * x, axis=-1, keepdims=True) + EPS) * g


def _logsig(z):
    return jnp.minimum(z, 0.0) - jnp.log(1.0 + jnp.exp(-jnp.abs(z)))


def _split_hi_lo(x):
    hi = x.astype(BF16)
    lo = (x - hi.astype(F32)).astype(BF16)
    return hi, lo


def _sel_dot(sel_bf16, x):
    hi, lo = _split_hi_lo(x)
    return _dot(sel_bf16, hi) + _dot(sel_bf16, lo)


def _dot_sel(x, sel_bf16):
    hi, lo = _split_hi_lo(x)
    return _dot(hi, sel_bf16) + _dot(lo, sel_bf16)


def _tile(x):
    return jnp.broadcast_to(x, (SUBLANES, LANES))


def _swiglu_into(h_scr, win_ref, wout_ref, act_scr):
    for j in range(D_FF // TF):
        a = _dot(h_scr[...], win_ref[:, j * TF:(j + 1) * TF])
        g = _dot(h_scr[...], win_ref[:, D_FF + j * TF:D_FF + (j + 1) * TF])
        act_scr[:, j * TF:(j + 1) * TF] = (a * jax.nn.sigmoid(a) * g).astype(BF16)
    return _dot(act_scr[...], wout_ref[...])


def _ffn1_kernel(x_ref, g_ref, win_ref, wout_ref, o_ref, h_scr, act_scr):
    x = x_ref[...]
    h_scr[...] = _rms(x, g_ref[...]).astype(BF16)
    o_ref[...] = x + 0.5 * _swiglu_into(h_scr, win_ref, wout_ref, act_scr)


def _outffn_kernel(x_ref, m_ref, wo_ref, g_ref, win_ref, wout_ref, gf_ref, o_ref, h_scr, act_scr, *, final_norm):
    x = x_ref[...]
    for hd in range(N_HEADS):
        x = x + _dot(m_ref[0, hd], wo_ref[hd * DV:(hd + 1) * DV, :])
    h_scr[...] = _rms(x, g_ref[...]).astype(BF16)
    x = x + 0.5 * _swiglu_into(h_scr, win_ref, wout_ref, act_scr)
    o_ref[...] = _rms(x, gf_ref[...]) if final_norm else x


def _inproj_kernel(x_ref, xp_ref, xn_ref, g_ref, wm_ref, ws_ref, nrm_ref, cw_ref, cb_ref, um_ref, sm_ref, h_scr):
    tm = x_ref.shape[1]
    h_scr[...] = _rms(x_ref[0], g_ref[...]).astype(BF16)
    i = pl.program_id(1)
    r16 = lax.broadcasted_iota(jnp.int32, (2 * SUBLANES, 1), 0)
    inside = jnp.where(r16 < SUBLANES, jnp.where(i > 0, 1.0, 0.0), jnp.where(i < pl.num_programs(1) - 1, 1.0, 0.0))
    h_halo = _rms(jnp.concatenate([xp_ref[0], xn_ref[0]], axis=0), g_ref[...]).astype(BF16)
    lane_q = lax.broadcasted_iota(jnp.int32, (1, TN_IN), 1) % (2 * DK) < DK
    q_scale = jnp.where(lane_q, QK_SCALE, 1.0)

    def store(col0, val):
        for k in range(TN_IN // DV):
            um_ref[0, col0 // DV + k] = val[:, k * DV:(k + 1) * DV].astype(BF16)

    def plain(name, j):
        cols = slice(GROUPS.index(name) * GROUP_W + j * TN_IN, GROUPS.index(name) * GROUP_W + (j + 1) * TN_IN)
        store(cols.start, _dot(h_scr[...], wm_ref[:, cols]))

    def conv(name, j):
        cols = slice(GROUPS.index(name) * GROUP_W + j * TN_IN, GROUPS.index(name) * GROUP_W + (j + 1) * TN_IN)
        cc = slice(j * TN_IN, (j + 1) * TN_IN)
        u = _dot(h_scr[...], wm_ref[:, cols])
        halo = _dot(h_halo, wm_ref[:, cols]) * inside
        xe = jnp.concatenate([halo[:SUBLANES], u, halo[SUBLANES:]], axis=0)
        prev = pltpu.roll(xe, 1, 0)[SUBLANES:SUBLANES + tm]
        nxt = pltpu.roll(xe, tm + 2 * SUBLANES - 1, 0)[SUBLANES:SUBLANES + tm]
        cv = cb_ref[:, cc] + prev * cw_ref[0:1, cc] + u * cw_ref[1:2, cc] + nxt * cw_ref[2:3, cc]
        store(cols.start, cv * jax.nn.sigmoid(cv) * q_scale)

    def gates(br, j):
        out_gate, branch_gate = ((PLAIN_GROUPS + 2 * br + t) * GROUP_W + j * TN_IN for t in (0, 1))
        u = _dot(h_scr[...], wm_ref[:, out_gate:out_gate + TN_IN])
        out_act = u * jax.nn.sigmoid(u) if br == 0 else jax.nn.sigmoid(u)
        w = out_act * jax.nn.sigmoid(_dot(h_scr[...], wm_ref[:, branch_gate:branch_gate + TN_IN]))
        dst = (PLAIN_GROUPS + br) * GROUP_W + j * TN_IN
        store(dst, w * nrm_ref[br:br + 1, j * TN_IN:(j + 1) * TN_IN])

    n_j = GROUP_W // TN_IN
    heavy_tasks = [t for j in range(n_j) for t in ((conv, "mqk", j), (gates, 0, j), (gates, 1, j))]
    light_tasks = [(name, j) for j in range(n_j) for name in ("gqk", "gv", "mv")]
    for heavy, light in zip(heavy_tasks, light_tasks):
        heavy[0](*heavy[1:])
        plain(*light)
    sm_ref[0] = _dot(h_scr[...], ws_ref[...])


def _mixer_kernel(gqk_ref, gv_ref, mqk_ref, mv_ref, wa_ref, wb_ref, sm_ref,
                  w2_ref, b2_ref, selrow_ref, bifrow_ref,
                  out_ref,
                  bcum_scr, att_scr, qin_scr, kst_t_scr, dcol_scr, gt_scr, grow_scr, bt_scr, qkt_scr, kt_scr,
                  a_scr, g_scr, mprev_scr, mnew_scr,
                  sml_scr, qw_scr, ks_t_scr, floor_scr, dint_scr, dec_scr, dn_scr, og_scr, hm_scr, s_scr, c_scr, n_scr,
                  sbf_scr, cbf_scr):
    gqk_ref, gv_ref, mqk_ref, mv_ref, wa_ref, wb_ref, out_ref = (
        r.at[0] for r in (gqk_ref, gv_ref, mqk_ref, mv_ref, wa_ref, wb_ref, out_ref))
    seq = gqk_ref.shape[1]
    nch = seq // CHUNK
    ri = lax.broadcasted_iota(jnp.int32, (CHUNK, CHUNK), 0)
    ci = lax.broadcasted_iota(jnp.int32, (CHUNK, CHUNK), 1)
    lower = ci <= ri
    upper = ci >= ri
    eye = ci == ri
    lower_bf = lower.astype(BF16)
    upper_bf = upper.astype(BF16)
    dirs = ((lower, GLA_MID - 1, CHUNK - 1), (upper, GLA_MID, 0))

    def chunk_rows(c):
        return pl.ds(pl.multiple_of(c * CHUNK, CHUNK), CHUNK)

    def to_col(row):
        return jnp.sum(jnp.where(eye, row, 0.0), axis=1, keepdims=True)

    def act_body(j, carry):
        rows = pl.ds(pl.multiple_of(j * GATE_ROWS, GATE_ROWS), GATE_ROWS)
        sm = sm_ref[0, rows, :]
        hm_scr[0, rows, :] = _logsig(_dot(sm.astype(BF16), w2_ref[0]) + b2_ref[0]) * (LOG2E / GLA_TAU)
        hi, lo = _split_hi_lo(sm)
        g_t = _dot_nt(selrow_ref[0], hi) + _dot_nt(selrow_ref[0], lo) + bifrow_ref[0]
        is_forget = lax.broadcasted_iota(jnp.int32, (SUBLANES, GATE_ROWS), 0) % 2 == 1
        g_t = jnp.where(is_forget, _logsig(g_t), g_t) * LOG2E
        for jj in range(GATE_ROWS // CHUNK):
            gt_scr[j * (GATE_ROWS // CHUNK) + jj] = g_t[:, jj * CHUNK:(jj + 1) * CHUNK]
        return carry

    lax.fori_loop(0, seq // GATE_ROWS, act_body, 0, unroll=PAR_UNROLL)

    row8 = lax.broadcasted_iota(jnp.int32, (SUBLANES, CHUNK), 0)

    def gate_chunk(c, slot):
        rows = chunk_rows(c)
        bcum_scr[slot, 0] = _sel_dot(lower_bf, hm_scr[0, rows, :DK])
        bcum_scr[slot, 1] = _sel_dot(upper_bf, hm_scr[0, rows, DK:])
        r = gt_scr[c]
        cum_f = _dot_sel(r, upper_bf)
        cum_b = _dot_sel(r, lower_bf)
        i_f, b_f, i_b, b_b = r[0:1], cum_f[1:2], r[2:3], cum_b[3:4]
        c_f = i_f - b_f
        c_b = i_b - b_b
        g8 = jnp.zeros((SUBLANES, CHUNK), F32)
        for idx, val in enumerate((i_f, b_f, i_b, b_b, c_f, c_b)):
            g8 = jnp.where(row8 == idx, val, g8)
        grow_scr[c] = g8
        for d, b, cc in ((0, b_f, c_f), (1, b_b, c_b)):
            l_idx = dirs[d][2]
            a = b[:, l_idx:l_idx + 1]
            a_scr[d, c] = _tile(a)
            g_scr[d, c] = _tile(a + jnp.max(cc, axis=1, keepdims=True))
            bt_scr[d, rows, :] = jnp.broadcast_to(to_col(b), (CHUNK, LANES))

    def gla_chunk(c, slot):
        rows = chunk_rows(c)
        qk = gqk_ref[0, rows, :].astype(F32)
        q = qk[:, :DK] * QK_SCALE
        k = qk[:, DK:]
        for d in (0, 1):
            mask, r_idx, l_idx = dirs[d]
            b = bcum_scr[slot, d]
            rho = b[r_idx:r_idx + 1, :]
            b_last = b[l_idx:l_idx + 1, :]
            q_mid = (q * jnp.exp2(b - rho)).astype(BF16)
            k_mid = (k * jnp.exp2(rho - b)).astype(BF16)
            att_scr[d, rows, :] = jnp.where(mask, _dot_nt(q_mid, k_mid), 0.0).astype(BF16)
            qin_scr[d, rows, :] = (q * jnp.exp2(b)).astype(BF16)
            kst_t_scr[d, c] = (k * jnp.exp2(b_last - b)).astype(BF16).T
            dcol_scr[d, c] = jnp.broadcast_to(to_col(jnp.exp2(b_last)), (DK, LANES))
        mqk = mqk_ref[0, rows, :]
        qkt_scr[rows, :] = _dot_nt(mqk[:, :DK], mqk[:, DK:])
        kt_scr[c] = mqk[:, DK:].T

    def pipelined(produce, consume, group):
        for j in range(group):
            produce(j, j)

        def body(q, carry):
            for half in (0, 1):
                base = (2 * q + half) * group
                for j in range(group):
                    produce(jnp.minimum(base + group + j, nch - 1), (1 - half) * group + j)
                for j in range(group):
                    consume(base + j, half * group + j)
            return carry

        lax.fori_loop(0, nch // (2 * group), body, 0)

    pipelined(gate_chunk, gla_chunk, GLA_GROUP)

    def m_body(i, carry):
        m_f, m_b = carry
        cb = nch - 1 - i
        mprev_scr[0, i] = m_f
        mprev_scr[1, cb] = m_b
        m_f = jnp.maximum(a_scr[0, i] + m_f, g_scr[0, i])
        m_b = jnp.maximum(a_scr[1, cb] + m_b, g_scr[1, cb])
        mnew_scr[0, i] = m_f
        mnew_scr[1, cb] = m_b
        return m_f, m_b

    m0 = jnp.full((SUBLANES, LANES), -jnp.inf, F32)
    lax.fori_loop(0, nch, m_body, (m0, m0))

    def prep_dir(d, c, slot):
        rows = chunk_rows(c)
        mask = dirs[d][0]
        g8 = grow_scr[c]
        b_t = bt_scr[d, rows, :]
        c_s = g8[4 + d:5 + d, :]
        m_prev = mprev_scr[d, c][0:1, 0:1]
        m_new = mnew_scr[d, c][0:1, 0:1]
        b_last = a_scr[d, c][0:1, 0:1]
        log_d = jnp.where(mask, b_t + c_s, -jnp.inf)
        m_inter = b_t + m_prev
        m_t = jnp.maximum(m_inter, jnp.max(log_d, axis=1, keepdims=True))
        sml = qkt_scr[rows, :] * jnp.exp2(log_d - m_t)
        sml_bf = sml.astype(BF16)
        sml_scr[slot, d] = sml_bf
        dint_scr[slot, d] = _dot(sml_bf, jnp.ones((CHUNK, LANES), BF16))
        qw_scr[slot, d] = (jnp.exp2(m_inter - m_t) * mqk_ref[0, rows, :DK].astype(F32)).astype(BF16)
        floor_scr[slot, d] = jnp.exp2(-m_t)
        w_st = jnp.broadcast_to(jnp.exp2(b_last + c_s - m_new), (SUBLANES, CHUNK))
        ks_t_scr[slot, d] = (kt_scr[c].astype(F32) * w_st[0:1, :]).astype(BF16)
        dn_scr[slot, d] = _dot(w_st.astype(BF16), mqk_ref[0, rows, DK:])
        dec_scr[slot, d] = _tile(jnp.exp2(b_last + m_prev - m_new))

    def scan_dir(d, c, slot):
        rows = chunk_rows(c)
        v = gv_ref[0, rows, :]
        og_scr[d, rows, :] = _dot(jnp.concatenate([att_scr[d, rows, :], qin_scr[d, rows, :]], axis=1),
                                  jnp.concatenate([v, sbf_scr[d]], axis=0))
        dcol = dcol_scr[d, c]
        s_new = s_scr[d] * jnp.concatenate([dcol, dcol], axis=1) + _dot(kst_t_scr[d, c], v)
        s_scr[d] = s_new
        sbf_scr[d] = s_new.astype(BF16)
        mv = mv_ref[0, rows, :]
        nst = n_scr[d]
        qw = qw_scr[slot, d]
        num = _dot(jnp.concatenate([sml_scr[slot, d], qw], axis=1), jnp.concatenate([mv, cbf_scr[d]], axis=0))
        den = dint_scr[slot, d] + jnp.sum(qw.astype(F32) * nst[0:1, :], axis=1, keepdims=True)
        rden = 1.0 / jnp.maximum(jnp.abs(den), floor_scr[slot, d])
        hm_scr[d, rows, :] = num * jnp.concatenate([rden, rden], axis=1)
        decay = dec_scr[slot, d][0:1, 0:1]
        c_new = decay * c_scr[d] + _dot(ks_t_scr[slot, d], mv)
        c_scr[d] = c_new
        cbf_scr[d] = c_new.astype(BF16)
        n_scr[d] = decay * nst + dn_scr[slot, d]

    for state in (s_scr, sbf_scr, c_scr, cbf_scr, n_scr):
        state[...] = jnp.zeros_like(state)

    def prep_step(i, slot):
        prep_dir(0, i, slot)
        prep_dir(1, nch - 1 - i, slot)

    def scan_step(i, slot):
        scan_dir(0, i, slot)
        scan_dir(1, nch - 1 - i, slot)

    pipelined(prep_step, scan_step, SCAN_GROUP)

    def merge_body(c, carry):
        rows = chunk_rows(c)
        og = og_scr[0, rows, :] + og_scr[1, rows, :]
        hm = hm_scr[0, rows, :] + hm_scr[1, rows, :]
        o_a = og * lax.rsqrt(jnp.mean(og * og, axis=-1, keepdims=True) + EPS)
        o_b = hm * lax.rsqrt(jnp.mean(hm * hm, axis=-1, keepdims=True) + EPS)
        merged = o_a * wa_ref[0, rows, :].astype(F32) + o_b * wb_ref[0, rows, :].astype(F32)
        out_ref[0, rows, :] = merged.astype(BF16)
        return carry

    lax.fori_loop(0, nch, merge_body, 0, unroll=PAR_UNROLL)


def _resident(shape):
    nd = len(shape)
    return pl.BlockSpec(shape, lambda *_: (0,) * nd, pipeline_mode=pl.Buffered(1))


def _params(n_axes):
    return pltpu.CompilerParams(dimension_semantics=("arbitrary",) * n_axes, vmem_limit_bytes=VMEM_LIMIT)


def _ffn1(x2d, g, win, wout):
    m = x2d.shape[0]
    tile = pl.BlockSpec((TM_FFN, D_MODEL), lambda i: (i, 0))
    return pl.pallas_call(
        _ffn1_kernel,
        grid=(m // TM_FFN,),
        in_specs=[tile, _resident(g.shape), _resident(win.shape), _resident(wout.shape)],
        out_specs=tile,
        out_shape=jax.ShapeDtypeStruct((m, D_MODEL), F32),
        scratch_shapes=[pltpu.VMEM((TM_FFN, D_MODEL), BF16), pltpu.VMEM((TM_FFN, D_FF), BF16)],
        compiler_params=_params(1),
        name="ffn1",
    )(x2d, g, win, wout)


def _outffn(x2d, merged, wo, g, win, wout, gf, final_norm):
    m = x2d.shape[0]
    per_seq = merged.shape[2] // TM_FFN
    tile = pl.BlockSpec((TM_FFN, D_MODEL), lambda i: (i, 0))
    heads = pl.BlockSpec((1, N_HEADS, TM_FFN, DV), lambda i: (i // per_seq, 0, i % per_seq, 0))
    return pl.pallas_call(
        functools.partial(_outffn_kernel, final_norm=final_norm),
        grid=(m // TM_FFN,),
        in_specs=[tile, heads, _resident(wo.shape), _resident(g.shape), _resident(win.shape), _resident(wout.shape),
                  _resident(gf.shape)],
        out_specs=tile,
        out_shape=jax.ShapeDtypeStruct((m, D_MODEL), F32),
        scratch_shapes=[pltpu.VMEM((TM_FFN, D_MODEL), BF16), pltpu.VMEM((TM_FFN, D_FF), BF16)],
        compiler_params=_params(1),
        name="outffn",
    )(x2d, merged, wo, g, win, wout, gf)


def _inproj(x, g, wm, ws, nrm, cw, cb):
    bsz, seq, _ = x.shape
    tm = TM_IN
    per_tile = tm // SUBLANES
    last = seq // SUBLANES - 1
    return pl.pallas_call(
        _inproj_kernel,
        grid=(bsz, seq // tm),
        in_specs=[pl.BlockSpec((1, tm, D_MODEL), lambda b, i: (b, i, 0)),
                  pl.BlockSpec((1, SUBLANES, D_MODEL), lambda b, i: (b, jnp.maximum(i * per_tile - 1, 0), 0)),
                  pl.BlockSpec((1, SUBLANES, D_MODEL), lambda b, i: (b, jnp.minimum((i + 1) * per_tile, last), 0)),
                  _resident(g.shape), _resident(wm.shape), _resident(ws.shape), _resident(nrm.shape),
                  _resident(cw.shape), _resident(cb.shape)],
        out_specs=[pl.BlockSpec((1, N_MAIN // DV, tm, DV), lambda b, i: (b, 0, i, 0)),
                   pl.BlockSpec((1, tm, LANES), lambda b, i: (b, i, 0))],
        out_shape=[jax.ShapeDtypeStruct((bsz, N_MAIN // DV, seq, DV), BF16),
                   jax.ShapeDtypeStruct((bsz, seq, LANES), F32)],
        scratch_shapes=[pltpu.VMEM((tm, D_MODEL), BF16)],
        compiler_params=_params(2),
        name="inproj",
    )(x, x, x, g, wm, ws, nrm, cw, cb)


def _mixer(um, sm, mp):
    bsz, _, seq, _ = um.shape
    nch = seq // CHUNK

    def group(name):
        base = GROUPS.index(name) * N_HEADS
        return pl.BlockSpec((1, 1, seq, DV), lambda b, h: (b, base + h, 0, 0))

    def head(arr):
        nd = arr.ndim
        return pl.BlockSpec((1,) + arr.shape[1:], lambda b, h: (h,) + (0,) * (nd - 1))

    plist = [mp[n] for n in ("w2", "b2", "selrow", "bifrow")]
    in_specs = [group(n) for n in GROUPS] + [pl.BlockSpec((1, seq, LANES), lambda b, h: (b, 0, 0))] \
        + [head(p) for p in plist]
    tok_bf = pltpu.VMEM((2, seq, DK), BF16)
    chunk_t_bf = pltpu.VMEM((2, nch, DK, CHUNK), BF16)
    chunk_tile = pltpu.VMEM((2, nch, SUBLANES, LANES), F32)
    slot_bf = pltpu.VMEM((PIPE_SLOTS, 2, CHUNK, CHUNK), BF16)
    slot_f32 = pltpu.VMEM((PIPE_SLOTS, 2, CHUNK, LANES), F32)
    slot_tile = pltpu.VMEM((PIPE_SLOTS, 2, SUBLANES, LANES), F32)
    return pl.pallas_call(
        _mixer_kernel,
        grid=(bsz, N_HEADS),
        in_specs=in_specs,
        out_specs=pl.BlockSpec((1, 1, seq, DV), lambda b, h: (b, h, 0, 0)),
        out_shape=jax.ShapeDtypeStruct((bsz, N_HEADS, seq, DV), BF16),
        scratch_shapes=[slot_f32,
                        tok_bf, tok_bf, chunk_t_bf,
                        pltpu.VMEM((2, nch, DK, LANES), F32),
                        pltpu.VMEM((nch, SUBLANES, CHUNK), F32),
                        pltpu.VMEM((nch, SUBLANES, CHUNK), F32),
                        pltpu.VMEM((2, seq, LANES), F32),
                        pltpu.VMEM((seq, CHUNK), F32),
                        pltpu.VMEM((nch, DK, CHUNK), BF16),
                        chunk_tile, chunk_tile, chunk_tile, chunk_tile,
                        slot_bf, slot_bf, slot_bf,
                        slot_f32, slot_f32,
                        slot_tile, slot_tile,
                        pltpu.VMEM((2, seq, DV), F32), pltpu.VMEM((2, seq, DV), F32),
                        pltpu.VMEM((2, DK, DV), F32), pltpu.VMEM((2, DK, DV), F32),
                        pltpu.VMEM((2, SUBLANES, DK), F32),
                        pltpu.VMEM((2, DK, DV), BF16), pltpu.VMEM((2, DK, DV), BF16)],
        compiler_params=_params(2),
        name="mixer",
    )(*([um] * len(GROUPS)), sm, *plist)


def _prep_layer(w_in, gla_w2_fwd, gla_b2_fwd, gla_w2_bwd, gla_b2_bwd, gla_norm, ml_conv_w, ml_conv_b, ml_b_if,
                ml_norm):
    sizes = (512, 512, 1024, 1024, 32, 1024, 1024, 1024, 16, 1024, 1024)
    offs = [0]
    for s in sizes:
        offs.append(offs[-1] + s)
    gq, gk, gv, gr, glr, mqk, mv, mo, mif, gate_a, gate_b = [w_in[:, offs[i]:offs[i + 1]] for i in range(len(sizes))]

    def pair(a, b):
        hd = lambda w: w.reshape(D_MODEL, N_HEADS, DK)
        return jnp.concatenate([hd(a), hd(b)], axis=2).reshape(D_MODEL, N_HEADS * 2 * DK)

    parts = dict(gqk=pair(gq, gk), gv=gv, gr=gr, mqk=pair(mqk[:, :512], mqk[:, 512:]), mv=mv, mo=mo, ga=gate_a,
                 gb=gate_b)
    wm = jnp.concatenate([parts[n] for n in GROUPS[:PLAIN_GROUPS] + W_GATES], axis=1).astype(BF16)
    nrm = jnp.stack([gla_norm, ml_norm])
    ws = jnp.concatenate([glr, mif, jnp.zeros((D_MODEL, LANES - 48), F32)], axis=1).astype(BF16)

    hd = lambda a: a.reshape(a.shape[:-1] + (N_HEADS, DK))

    def w2pad(w2, row0):
        w = jnp.moveaxis(hd(w2), 1, 0)
        return jnp.pad(w, ((0, 0), (row0, LANES - row0 - GLA_LOWRANK), (0, 0)))

    w2 = jnp.concatenate([w2pad(gla_w2_fwd, SM_LR_F), w2pad(gla_w2_bwd, SM_LR_B)], axis=2).astype(BF16)
    b2 = jnp.concatenate([hd(gla_b2_fwd), hd(gla_b2_bwd)], axis=1)[:, None, :]

    g_idx = jnp.arange(4)
    h_idx = jnp.arange(N_HEADS)
    src = SM_IF + 4 * g_idx[None, :] + h_idx[:, None]
    selrow = jnp.zeros((N_HEADS, SUBLANES, LANES), F32).at[h_idx[:, None], g_idx[None, :], src].set(1.0)
    bif = jnp.pad(ml_b_if.reshape(4, N_HEADS).T, ((0, 0), (0, SUBLANES - 4)))
    bifrow = jnp.broadcast_to(bif[:, :, None], (N_HEADS, SUBLANES, GATE_ROWS))

    cw = jnp.concatenate([hd(ml_conv_w[:, :512]), hd(ml_conv_w[:, 512:])], axis=2).reshape(3, GROUP_W)
    cb = jnp.concatenate([hd(ml_conv_b[:512]), hd(ml_conv_b[512:])], axis=1).reshape(1, GROUP_W)
    mp = dict(w2=w2, b2=b2, selrow=selrow.astype(BF16), bifrow=bifrow)
    return wm, ws, nrm, cw, cb, mp


def kernel(x_prompt, x_sample, g_ffn1, w_ffn1_in, w_ffn1_out, g_mix, w_in, gla_w2_fwd, gla_b2_fwd, gla_w2_bwd,
           gla_b2_bwd, gla_norm, ml_conv_w, ml_conv_b, ml_b_if, ml_norm, w_out, g_ffn2, w_ffn2_in, w_ffn2_out,
           g_final):
    depth = w_in.shape[0]
    layers = []
    for l in range(depth):
        layers.append(dict(
            g1=g_ffn1[l][None, :], w1i=w_ffn1_in[l].astype(BF16), w1o=w_ffn1_out[l].astype(BF16),
            gm=g_mix[l][None, :],
            mix=_prep_layer(w_in[l], gla_w2_fwd[l], gla_b2_fwd[l], gla_w2_bwd[l], gla_b2_bwd[l], gla_norm[l],
                            ml_conv_w[l], ml_conv_b[l], ml_b_if[l], ml_norm[l]),
            wo=w_out[l].astype(BF16), g2=g_ffn2[l][None, :], w2i=w_ffn2_in[l].astype(BF16),
            w2o=w_ffn2_out[l].astype(BF16)))
    gf = g_final[None, :]

    def trunk(x):
        bsz, seq, _ = x.shape
        for l, p in enumerate(layers):
            wm, ws, nrm, cw, cb, mp = p["mix"]
            x1 = _ffn1(x.reshape(bsz * seq, D_MODEL), p["g1"], p["w1i"], p["w1o"])
            um, sm = _inproj(x1.reshape(bsz, seq, D_MODEL), p["gm"], wm, ws, nrm, cw, cb)
            merged = _mixer(um, sm, mp)
            x = _outffn(x1, merged, p["wo"], p["g2"], p["w2i"], p["w2o"], gf,
                        final_norm=(l == depth - 1))
            x = x.reshape(bsz, seq, D_MODEL)
        return x

    return trunk(x_prompt), trunk(x_sample)
```
